```python
import jax, jax.numpy as jnp
from jax import lax
import numpy as np

D_MODEL = 1024
BATCH = 32
SEQ = 2048
DEPTH = 2
DEC_BATCH = 128
DEC_SEQ = 8
PAST_LEN = 16384
PAGE_SIZE = 128

N_MIXERS = 2
N_MLA_LAYERS = (DEPTH + 1) // 2
N_NSA_LAYERS = DEPTH // 2
NORM_EPS = 1e-6
ROPE_THETA = 10000.0
Q_BLOCK = 128
MLA_HEADS = 8
MLA_Q_LORA = 512
MLA_KV_LORA = 256
MLA_NOPE = 128
MLA_ROPE = 64
MLA_V = 128
MLA_LAT = MLA_KV_LORA + MLA_ROPE
MLA_SCALE = (MLA_NOPE + MLA_ROPE) ** -0.5
NSA_HEADS = 16
NSA_KV_HEADS = 4
NSA_GROUP = NSA_HEADS // NSA_KV_HEADS
NSA_HD = 64
NSA_CMP_LEN = 32
NSA_CMP_STRIDE = 16
NSA_CMP_RATIO = NSA_CMP_LEN // NSA_CMP_STRIDE
NSA_SLC_LEN = 64
NSA_TOPN = 16
NSA_WINDOW = 512
NSA_PHI_HIDDEN = 128
NSA_Q_BLOCK = 64
NSA_SCALE = NSA_HD ** -0.5
NSA_FORCE = 1e9
NSA_IN = NSA_HEADS * NSA_HD + 6 * NSA_KV_HEADS * NSA_HD + 3 * NSA_HEADS
PEER_HEADS = 8
PEER_NKEYS = 128
PEER_EXPERTS = PEER_NKEYS * PEER_NKEYS
PEER_DKEY = 256
PEER_TOPK = 16
PEER_CHUNK = 256

kernel_name = "mla_nsa_peer_hybrid_step"


def rms_norm(x, g):
    xf = x.astype(jnp.float32)
    y = xf * lax.rsqrt(jnp.mean(xf * xf, -1, keepdims=True) + NORM_EPS)
    return (y * g.astype(jnp.float32)).astype(x.dtype)


def rope(x, pos):
    half = x.shape[-1] // 2
    inv = ROPE_THETA ** (-jnp.arange(half, dtype=jnp.float32) / half)
    ang = pos.astype(jnp.float32)[:, None] * inv[None, :]
    shp = (pos.shape[0],) + (1,) * (x.ndim - 3) + (half,)
    cos, sin = jnp.cos(ang).reshape(shp), jnp.sin(ang).reshape(shp)
    xf = x.astype(jnp.float32)
    x1, x2 = xf[..., :half], xf[..., half:]
    return jnp.concatenate([x1 * cos - x2 * sin, x2 * cos + x1 * sin], -1).astype(x.dtype)


def masked_softmax(s, mask):
    s = jnp.where(mask, s.astype(jnp.float32), -jnp.inf)
    m = jnp.max(s, -1, keepdims=True)
    m = jnp.where(jnp.isfinite(m), m, 0.0)
    p = jnp.exp(s - m)
    den = jnp.sum(p, -1, keepdims=True)
    return p / jnp.where(den > 0, den, 1.0)


def map_query_blocks(fn, qs, q_pos, blk):
    T = q_pos.shape[0]
    if T <= blk:
        return fn(qs, q_pos)
    nb = -(-T // blk)
    pad = nb * blk - T

    def split(a):
        a = jnp.pad(a, [(0, 0), (0, pad)] + [(0, 0)] * (a.ndim - 2))
        return jnp.moveaxis(a.reshape(a.shape[0], nb, blk, *a.shape[2:]), 1, 0)

    qb = tuple(split(a) for a in qs)
    pb = jnp.pad(q_pos, (0, pad), mode='edge').reshape(nb, blk)
    out = lax.map(lambda a: fn(a[0], a[1]), (qb, pb))
    out = jnp.moveaxis(out, 0, 1)
    return out.reshape(out.shape[0], nb * blk, *out.shape[3:])[:, :T]


def mla_project(h, pos, w_in, q_norm, kv_norm, w_uq, w_uk):
    z = h @ w_in
    cq = rms_norm(z[..., :MLA_Q_LORA], q_norm)
    ckv = rms_norm(z[..., MLA_Q_LORA:MLA_Q_LORA + MLA_KV_LORA], kv_norm)
    kr = rope(z[..., MLA_Q_LORA + MLA_KV_LORA:][:, :, None, :], pos)[:, :, 0]
    lat = jnp.concatenate([ckv, kr], -1)
    q = jnp.einsum('btr,rhd->bthd', cq, w_uq)
    q_lat = jnp.einsum('bthn,rhn->bthr', q[..., :MLA_NOPE], w_uk)
    q_abs = jnp.concatenate([q_lat, rope(q[..., MLA_NOPE:], pos)], -1)
    return lat, q_abs


def latent_attend(q_abs, q_pos, lat, k_pos):
    s = jnp.einsum('bthc,bsc->bhts', q_abs, lat) * MLA_SCALE
    p = masked_softmax(s, (k_pos[None, :] <= q_pos[:, None])[None, None])
    return jnp.einsum('bhts,bsr->bthr', p.astype(lat.dtype), lat[..., :MLA_KV_LORA])


def mla_out(o_lat, w_uv, w_o):
    B, T = o_lat.shape[:2]
    o = jnp.einsum('bthr,rhv->bthv', o_lat, w_uv)
    return o.reshape(B, T, MLA_HEADS * MLA_V) @ w_o


def mla_prompt(h, w_in, q_norm, kv_norm, w_uq, w_uk, w_uv, w_o):
    T = h.shape[1]
    pos = jnp.arange(T, dtype=jnp.int32)
    lat, q_abs = mla_project(h, pos, w_in, q_norm, kv_norm, w_uq, w_uk)
    o = map_query_blocks(lambda qs, p: latent_attend(qs[0], p, lat, pos), (q_abs,), pos, Q_BLOCK)
    return mla_out(o, w_uv, w_o), lat


def mla_sample(h, pool, page_table, w_in, q_norm, kv_norm, w_uq, w_uk, w_uv, w_o):
    B, T = h.shape[:2]
    past = page_table.shape[1] * PAGE_SIZE
    pos = past + jnp.arange(T, dtype=jnp.int32)
    lat_new, q_abs = mla_project(h, pos, w_in, q_norm, kv_norm, w_uq, w_uk)
    lat = jnp.concatenate([pool[page_table].reshape(B, past, MLA_LAT), lat_new], 1)
    k_pos = jnp.arange(past + T, dtype=jnp.int32)
    o = map_query_blocks(lambda qs, p: latent_attend(qs[0], p, lat, k_pos), (q_abs,), pos, Q_BLOCK)
    return mla_out(o, w_uv, w_o), lat_new


def nsa_project(h, pos, w_in, b_gate):
    B, T = h.shape[:2]
    nq = NSA_HEADS * NSA_HD
    nkv = 6 * NSA_KV_HEADS * NSA_HD
    z = h @ w_in
    q = z[..., :nq].reshape(B, T, NSA_KV_HEADS, NSA_GROUP, NSA_HD)
    kv = z[..., nq:nq + nkv].reshape(B, T, 3, 2, NSA_KV_HEADS, NSA_HD)
    gates = jax.nn.sigmoid(z[..., nq + nkv:] + b_gate).reshape(B, T, NSA_KV_HEADS, NSA_GROUP, 3)
    q_rot = rope(q, pos)

    def rot_k(r):
        return jnp.stack([rope(r[:, :, 0], pos), r[:, :, 1]], axis=2)

    return q, q_rot, gates, kv[:, :, 0], rot_k(kv[:, :, 1]), rot_k(kv[:, :, 2])


def nsa_compress(rows, w1, b1, w2):
    B, T = rows.shape[:2]
    nch = -(-T // NSA_CMP_STRIDE)
    rows = jnp.pad(rows, ((0, 0), (0, nch * NSA_CMP_STRIDE - T), (0, 0), (0, 0)))
    ch = rows.reshape(B, nch, NSA_CMP_STRIDE, NSA_KV_HEADS, NSA_HD)
    nb = nch - NSA_CMP_RATIO + 1
    pre = b1
    for r in range(NSA_CMP_RATIO):
        w1r = w1[r * NSA_CMP_STRIDE:(r + 1) * NSA_CMP_STRIDE]
        pre = pre + jnp.einsum('bclgd,ldh->bcgh', ch[:, r:r + nb], w1r)
    return jnp.einsum('bngh,hd->bngd', jax.nn.gelu(pre), w2)


def nsa_core(q_raw, q_rot, gates, q_pos, kc, vc, gather_slc, n_blocks, kw, vw, kw_pos):
    nb = kc.shape[1]
    cmp_end = jnp.arange(nb, dtype=jnp.int32) * NSA_CMP_STRIDE + NSA_CMP_LEN - 1
    s_c = jnp.einsum('btghd,bngd->btghn', q_raw, kc) * NSA_SCALE
    p_c = masked_softmax(s_c, (cmp_end[None, :] <= q_pos[:, None])[None, :, None, None, :])
    o_c = jnp.einsum('btghn,bngd->btghd', p_c.astype(vc.dtype), vc)
    ci = jnp.arange(nb, dtype=jnp.int32)[:, None] * NSA_CMP_STRIDE
    sj = jnp.arange(n_blocks, dtype=jnp.int32)[None, :] * NSA_SLC_LEN
    cover = ((ci < sj + NSA_SLC_LEN) & (ci + NSA_CMP_LEN > sj)).astype(jnp.float32)
    imp = jnp.einsum('btghn,nj->btgj', p_c, cover)
    j = jnp.arange(n_blocks, dtype=jnp.int32)[None, :]
    cur = (q_pos // NSA_SLC_LEN)[:, None]
    valid = j * NSA_SLC_LEN <= q_pos[:, None]
    forced = (j == 0) | (j == cur) | (j == cur - 1)
    score = jnp.where((valid & forced)[None, :, None, :], NSA_FORCE,
                      jnp.where(valid[None, :, None, :], imp, -jnp.inf))
    _, idx = lax.top_k(score, min(NSA_TOPN, n_blocks))
    sel = gather_slc(idx)
    ks, vs = sel[..., 0, :], sel[..., 1, :]
    k_pos = idx[..., None] * NSA_SLC_LEN + jnp.arange(NSA_SLC_LEN, dtype=jnp.int32)
    s_s = jnp.einsum('btghd,btgnld->btghnl', q_rot, ks) * NSA_SCALE
    B, Tq, G, H, n, L = s_s.shape
    m_s = (k_pos <= q_pos[None, :, None, None, None]).reshape(B, Tq, G, 1, n * L)
    p_s = masked_softmax(s_s.reshape(B, Tq, G, H, n * L), m_s).reshape(s_s.shape)
    o_s = jnp.einsum('btghnl,btgnld->btghd', p_s.astype(vs.dtype), vs)
    s_w = jnp.einsum('btghd,bsgd->btghs', q_rot, kw) * NSA_SCALE
    dpos = q_pos[:, None] - kw_pos[None, :]
    m_w = (dpos >= 0) & (dpos < NSA_WINDOW) & (kw_pos[None, :] >= 0)
    p_w = masked_softmax(s_w, m_w[None, :, None, None, :])
    o_w = jnp.einsum('btghs,bsgd->btghd', p_w.astype(vw.dtype), vw)
    return gates[..., 0:1] * o_c + gates[..., 1:2] * o_s + gates[..., 2:3] * o_w


def _bg(B):
    return (jnp.arange(B)[:, None, None, None], jnp.arange(NSA_KV_HEADS)[None, None, :, None])


def nsa_prompt(h, w_in, b_gate, w1, b1, w2, w_o):
    B, T = h.shape[:2]
    pos = jnp.arange(T, dtype=jnp.int32)
    q, q_rot, gates, cmp_rows, slc_rows, win_rows = nsa_project(h, pos, w_in, b_gate)
    kc = nsa_compress(cmp_rows[:, :, 0], w1[0], b1[0], w2[0])
    vc = nsa_compress(cmp_rows[:, :, 1], w1[1], b1[1], w2[1])
    n_blocks = T // NSA_SLC_LEN
    blocks = slc_rows.reshape(B, n_blocks, NSA_SLC_LEN, 2, NSA_KV_HEADS, NSA_HD)
    bi, gi = _bg(B)

    def gather(idx):
        return blocks[bi, idx, :, :, gi, :]

    kw_pad = jnp.pad(win_rows, ((0, 0), (NSA_WINDOW, 0), (0, 0), (0, 0), (0, 0)))

    def blockfn(qs, p):
        blk = p.shape[0]
        kwb = lax.dynamic_slice_in_dim(kw_pad, p[0], NSA_WINDOW + blk, axis=1)
        kwp = p[0] - NSA_WINDOW + jnp.arange(NSA_WINDOW + blk, dtype=jnp.int32)
        return nsa_core(qs[0], qs[1], qs[2], p, kc, vc, gather, n_blocks, kwb[:, :, 0], kwb[:, :, 1], kwp)

    o = map_query_blocks(blockfn, (q, q_rot, gates), pos, NSA_Q_BLOCK)
    y = o.reshape(B, T, NSA_HEADS * NSA_HD) @ w_o
    wb = min(NSA_WINDOW, T)
    return y, cmp_rows, slc_rows, win_rows[:, T - wb:]


def nsa_sample(h, pool_cmp, pool_slc, win_state, page_table, w_in, b_gate, w1, b1, w2, w_o):
    B, T = h.shape[:2]
    past = page_table.shape[1] * PAGE_SIZE
    pos = past + jnp.arange(T, dtype=jnp.int32)
    q, q_rot, gates, cmp_new, slc_new, win_new = nsa_project(h, pos, w_in, b_gate)
    cmp_all = jnp.concatenate([pool_cmp[page_table].reshape(B, past, 2, NSA_KV_HEADS, NSA_HD), cmp_new], 1)
    kc = nsa_compress(cmp_all[:, :, 0], w1[0], b1[0], w2[0])
    vc = nsa_compress(cmp_all[:, :, 1], w1[1], b1[1], w2[1])
    spp = PAGE_SIZE // NSA_SLC_LEN
    pool_blocks = pool_slc.reshape(pool_slc.shape[0] * spp, NSA_SLC_LEN, 2, NSA_KV_HEADS, NSA_HD)
    ns_past = past // NSA_SLC_LEN
    n_new = -(-T // NSA_SLC_LEN)
    new_blocks = jnp.pad(slc_new, ((0, 0), (0, n_new * NSA_SLC_LEN - T), (0, 0), (0, 0), (0, 0)))
    new_blocks = new_blocks.reshape(B, n_new, NSA_SLC_LEN, 2, NSA_KV_HEADS, NSA_HD)
    bi, gi = _bg(B)

    def gather(idx):
        jp = jnp.minimum(idx, ns_past - 1)
        phys = page_table[bi, jp // spp] * spp + jp % spp
        from_past = pool_blocks[phys, :, :, gi, :]
        jn = jnp.clip(idx - ns_past, 0, n_new - 1)
        from_new = new_blocks[bi, jn, :, :, gi, :]
        return jnp.where((idx < ns_past)[..., None, None, None], from_past, from_new)

    wb = win_state.shape[1]
    win_all = jnp.concatenate([win_state, win_new], 1)
    kw_pos = past - wb + jnp.arange(wb + T, dtype=jnp.int32)
    o = map_query_blocks(
        lambda qs, p: nsa_core(qs[0], qs[1], qs[2], p, kc, vc, gather, ns_past + n_new,
                               win_all[:, :, 0], win_all[:, :, 1], kw_pos),
        (q, q_rot, gates), pos, NSA_Q_BLOCK)
    y = o.reshape(B, T, NSA_HEADS * NSA_HD) @ w_o
    return y, cmp_new, slc_new, win_all[:, T:]


def peer(h, w_q, sub_keys, u, v):
    B, T, D = h.shape
    n = B * T
    pad = (-n) % PEER_CHUNK
    chunks = jnp.pad(h.reshape(n, D), ((0, pad), (0, 0))).reshape(-1, PEER_CHUNK, D)

    def one(xc):
        q = (xc @ w_q).reshape(PEER_CHUNK, PEER_HEADS, 2, PEER_DKEY // 2)
        s = jnp.einsum('chpk,pnk->chpn', q, sub_keys)
        s1, i1 = lax.top_k(s[:, :, 0], PEER_TOPK)
        s2, i2 = lax.top_k(s[:, :, 1], PEER_TOPK)
        cand = (s1[..., :, None] + s2[..., None, :]).reshape(PEER_CHUNK, PEER_HEADS, PEER_TOPK * PEER_TOPK)
        cidx = (i1[..., :, None] * PEER_NKEYS + i2[..., None, :]).reshape(PEER_CHUNK, PEER_HEADS, PEER_TOPK * PEER_TOPK)
        top_s, sel = lax.top_k(cand, PEER_TOPK)
        idx = jnp.take_along_axis(cidx, sel, -1)
        g = jax.nn.softmax(top_s.astype(jnp.float32), -1)
        a = jax.nn.gelu(jnp.einsum('cd,chkd->chk', xc, u[idx]).astype(jnp.float32))
        return jnp.einsum('chk,chkd->cd', (g * a).astype(v.dtype), v[idx])

    out = lax.map(one, chunks).reshape(-1, D)[:n]
    return out.reshape(B, T, D)


def setup_inputs(seed: int = 0) -> dict:
    key = jax.random.key(seed)
    ks = jax.random.split(key, 32)
    n_pages = PAST_LEN // PAGE_SIZE
    n_pool = (DEC_BATCH * n_pages * 5) // 4
    wb = min(NSA_WINDOW, PAST_LEN)
    f32 = jnp.float32
    nrm = lambda k, shp, sc: jax.random.normal(k, shp, f32) * sc
    gain = lambda k, shp: 1.0 + 0.01 * jax.random.normal(k, shp, f32)
    perm = jax.random.permutation(ks[6], n_pool)[:DEC_BATCH * n_pages]
    return {
        'x_prompt': nrm(ks[0], (BATCH, SEQ, D_MODEL), 1.0),
        'x_sample': nrm(ks[1], (DEC_BATCH, DEC_SEQ, D_MODEL), 1.0),
        'cache_mla_latent': nrm(ks[2], (N_MLA_LAYERS, n_pool, PAGE_SIZE, MLA_LAT), 1.0),
        'cache_nsa_cmp': nrm(ks[3], (N_NSA_LAYERS, n_pool, PAGE_SIZE, 2, NSA_KV_HEADS, NSA_HD), 1.0),
        'cache_nsa_slc': nrm(ks[4], (N_NSA_LAYERS, n_pool, PAGE_SIZE, 2, NSA_KV_HEADS, NSA_HD), 1.0),
        'state_nsa_win': nrm(ks[5], (N_NSA_LAYERS, DEC_BATCH, wb, 2, NSA_KV_HEADS, NSA_HD), 1.0),
        'page_table': perm.reshape(DEC_BATCH, n_pages).astype(jnp.int32),
        'norm_mix': gain(ks[7], (DEPTH, D_MODEL)),
        'norm_ffn': gain(ks[8], (DEPTH, D_MODEL)),
        'norm_final': gain(ks[9], (D_MODEL,)),
        'mla_w_in': nrm(ks[10], (N_MLA_LAYERS, D_MODEL, MLA_Q_LORA + MLA_KV_LORA + MLA_ROPE), D_MODEL ** -0.5),
        'mla_q_norm': gain(ks[11], (N_MLA_LAYERS, MLA_Q_LORA)),
        'mla_kv_norm': gain(ks[12], (N_MLA_LAYERS, MLA_KV_LORA)),
        'mla_w_uq': nrm(ks[13], (N_MLA_LAYERS, MLA_Q_LORA, MLA_HEADS, MLA_NOPE + MLA_ROPE), MLA_Q_LORA ** -0.5),
        'mla_w_uk': nrm(ks[14], (N_MLA_LAYERS, MLA_KV_LORA, MLA_HEADS, MLA_NOPE), MLA_KV_LORA ** -0.5),
        'mla_w_uv': nrm(ks[15], (N_MLA_LAYERS, MLA_KV_LORA, MLA_HEADS, MLA_V), MLA_KV_LORA ** -0.5),
        'mla_w_o': nrm(ks[16], (N_MLA_LAYERS, MLA_HEADS * MLA_V, D_MODEL), (MLA_HEADS * MLA_V) ** -0.5),
        'nsa_w_in': nrm(ks[17], (N_NSA_LAYERS, D_MODEL, NSA_IN), D_MODEL ** -0.5),
        'nsa_b_gate': nrm(ks[18], (N_NSA_LAYERS, 3 * NSA_HEADS), 0.1),
        'nsa_phi_w1': nrm(ks[19], (N_NSA_LAYERS, 2, NSA_CMP_LEN, NSA_HD, NSA_PHI_HIDDEN), (NSA_CMP_LEN * NSA_HD) ** -0.5),
        'nsa_phi_b1': nrm(ks[20], (N_NSA_LAYERS, 2, NSA_PHI_HIDDEN), 0.02),
        'nsa_phi_w2': nrm(ks[21], (N_NSA_LAYERS, 2, NSA_PHI_HIDDEN, NSA_HD), NSA_PHI_HIDDEN ** -0.5),
        'nsa_w_o': nrm(ks[22], (N_NSA_LAYERS, NSA_HEADS * NSA_HD, D_MODEL), (NSA_HEADS * NSA_HD) ** -0.5),
        'peer_w_q': nrm(ks[23], (DEPTH, D_MODEL, PEER_HEADS * PEER_DKEY), D_MODEL ** -0.5),
        'peer_sub_keys': nrm(ks[24], (DEPTH, 2, PEER_NKEYS, PEER_DKEY // 2), (PEER_DKEY // 2) ** -0.5),
        'peer_u': nrm(ks[25], (DEPTH, PEER_EXPERTS, D_MODEL), D_MODEL ** -0.5),
        'peer_v': nrm(ks[26], (DEPTH, PEER_EXPERTS, D_MODEL), PEER_TOPK ** -0.5),
    }


def reference(x_prompt, x_sample, cache_mla_latent, cache_nsa_cmp, cache_nsa_slc, state_nsa_win, page_table,
              norm_mix, norm_ffn, norm_final, mla_w_in, mla_q_norm, mla_kv_norm, mla_w_uq, mla_w_uk, mla_w_uv,
              mla_w_o, nsa_w_in, nsa_b_gate, nsa_phi_w1, nsa_phi_b1, nsa_phi_w2, nsa_w_o, peer_w_q, peer_sub_keys,
              peer_u, peer_v):
    xp, xs = x_prompt, x_sample
    mla_p, mla_s, cmp_p, cmp_s, slc_p, slc_s, win_p, win_s = [], [], [], [], [], [], [], []
    for i in range(DEPTH):
        j = i // N_MIXERS
        hp = rms_norm(xp, norm_mix[i])
        hs = rms_norm(xs, norm_mix[i])
        if i % N_MIXERS == 0:
            w = (mla_w_in[j], mla_q_norm[j], mla_kv_norm[j], mla_w_uq[j], mla_w_uk[j], mla_w_uv[j], mla_w_o[j])
            op, rp = mla_prompt(hp, *w)
            osm, rs = mla_sample(hs, cache_mla_latent[j], page_table, *w)
            mla_p.append(rp)
            mla_s.append(rs)
        else:
            w = (nsa_w_in[j], nsa_b_gate[j], nsa_phi_w1[j], nsa_phi_b1[j], nsa_phi_w2[j], nsa_w_o[j])
            op, cp, sp, wp = nsa_prompt(hp, *w)
            osm, cs, ss, ws = nsa_sample(hs, cache_nsa_cmp[j], cache_nsa_slc[j], state_nsa_win[j], page_table, *w)
            cmp_p.append(cp)
            cmp_s.append(cs)
            slc_p.append(sp)
            slc_s.append(ss)
            win_p.append(wp)
            win_s.append(ws)
        xp = xp + op
        xs = xs + osm
        pw = (peer_w_q[i], peer_sub_keys[i], peer_u[i], peer_v[i])
        xp = xp + peer(rms_norm(xp, norm_ffn[i]), *pw)
        xs = xs + peer(rms_norm(xs, norm_ffn[i]), *pw)
    y_prompt = rms_norm(xp, norm_final)
    y_sample = rms_norm(xs, norm_final)
    return (y_prompt, y_sample, jnp.stack(mla_p), jnp.stack(mla_s), jnp.stack(cmp_p), jnp.stack(cmp_s),
            jnp.stack(slc_p), jnp.stack(slc_s), jnp.stack(win_p), jnp.stack(win_s))
```

```python
import functools

import jax
import jax.numpy as jnp
import numpy as np
from jax import lax
from jax.experimental import pallas as pl
from jax.experimental.pallas import tpu as pltpu

D_MODEL = 1024
PAGE_SIZE = 128
N_MIXERS = 2
NORM_EPS = 1e-6
ROPE_THETA = 10000.0
Q_BLOCK = 128
MLA_HEADS = 8
MLA_Q_LORA = 512
MLA_KV_LORA = 256
MLA_NOPE = 128
MLA_ROPE = 64
MLA_V = 128
MLA_LAT = MLA_KV_LORA + MLA_ROPE
MLA_SCALE = (MLA_NOPE + MLA_ROPE) ** -0.5
NSA_HEADS = 16
NSA_KV_HEADS = 4
NSA_GROUP = NSA_HEADS // NSA_KV_HEADS
NSA_HD = 64
NSA_CMP_LEN = 32
NSA_CMP_STRIDE = 16
NSA_CMP_RATIO = NSA_CMP_LEN // NSA_CMP_STRIDE
NSA_SLC_LEN = 64
NSA_TOPN = 16
NSA_WINDOW = 512
NSA_Q_BLOCK = 64
NSA_SCALE = NSA_HD ** -0.5
NSA_FORCE = 1e9
PEER_HEADS = 8
PEER_NKEYS = 128
PEER_DKEY = 256
PEER_TOPK = 16
PEER_CHUNK = 256


def _rms_norm_body(x_ref, g_ref, o_ref):
    x = x_ref[...]
    y = x * lax.rsqrt(jnp.mean(x * x, -1, keepdims=True) + NORM_EPS)
    o_ref[...] = y * g_ref[...]


def rms_norm_pallas(x, g, rows=512):
    shp = x.shape
    x2 = x.reshape(-1, shp[-1])
    n, d = x2.shape
    rows = min(rows, n)
    out = pl.pallas_call(
        _rms_norm_body,
        grid=(n // rows,),
        in_specs=[pl.BlockSpec((rows, d), lambda i: (i, 0)),
                  pl.BlockSpec((1, d), lambda i: (0, 0))],
        out_specs=pl.BlockSpec((rows, d), lambda i: (i, 0)),
        out_shape=jax.ShapeDtypeStruct((n, d), x.dtype),
    )(x2, g.reshape(1, d))
    return out.reshape(shp)


def rms_norm(x, g):
    xf = x.astype(jnp.float32)
    y = xf * lax.rsqrt(jnp.mean(xf * xf, -1, keepdims=True) + NORM_EPS)
    return (y * g.astype(jnp.float32)).astype(x.dtype)


def rope(x, pos):
    half = x.shape[-1] // 2
    inv = ROPE_THETA ** (-jnp.arange(half, dtype=jnp.float32) / half)
    ang = pos.astype(jnp.float32)[:, None] * inv[None, :]
    shp = (pos.shape[0],) + (1,) * (x.ndim - 3) + (half,)
    cos, sin = jnp.cos(ang).reshape(shp), jnp.sin(ang).reshape(shp)
    xf = x.astype(jnp.float32)
    x1, x2 = xf[..., :half], xf[..., half:]
    return jnp.concatenate([x1 * cos - x2 * sin, x2 * cos + x1 * sin], -1).astype(x.dtype)


def masked_softmax(s, mask):
    s = jnp.where(mask, s.astype(jnp.float32), -jnp.inf)
    m = jnp.max(s, -1, keepdims=True)
    m = jnp.where(jnp.isfinite(m), m, 0.0)
    p = jnp.exp(s - m)
    den = jnp.sum(p, -1, keepdims=True)
    return p / jnp.where(den > 0, den, 1.0)


def map_query_blocks(fn, qs, q_pos, blk):
    T = q_pos.shape[0]
    if T <= blk:
        return fn(qs, q_pos)
    nb = -(-T // blk)
    pad = nb * blk - T

    def split(a):
        a = jnp.pad(a, [(0, 0), (0, pad)] + [(0, 0)] * (a.ndim - 2))
        return jnp.moveaxis(a.reshape(a.shape[0], nb, blk, *a.shape[2:]), 1, 0)

    qb = tuple(split(a) for a in qs)
    pb = jnp.pad(q_pos, (0, pad), mode='edge').reshape(nb, blk)
    out = lax.map(lambda a: fn(a[0], a[1]), (qb, pb))
    out = jnp.moveaxis(out, 0, 1)
    return out.reshape(out.shape[0], nb * blk, *out.shape[3:])[:, :T]


def mla_project(h, pos, w_in, q_norm, kv_norm, w_uq, w_uk):
    z = h @ w_in
    cq = rms_norm(z[..., :MLA_Q_LORA], q_norm)
    ckv = rms_norm(z[..., MLA_Q_LORA:MLA_Q_LORA + MLA_KV_LORA], kv_norm)
    kr = rope(z[..., MLA_Q_LORA + MLA_KV_LORA:][:, :, None, :], pos)[:, :, 0]
    lat = jnp.concatenate([ckv, kr], -1)
    q = jnp.einsum('btr,rhd->bthd', cq, w_uq)
    q_lat = jnp.einsum('bthn,rhn->bthr', q[..., :MLA_NOPE], w_uk)
    q_abs = jnp.concatenate([q_lat, rope(q[..., MLA_NOPE:], pos)], -1)
    return lat, q_abs


def latent_attend(q_abs, q_pos, lat, k_pos):
    s = jnp.einsum('bthc,bsc->bhts', q_abs, lat) * MLA_SCALE
    p = masked_softmax(s, (k_pos[None, :] <= q_pos[:, None])[None, None])
    return jnp.einsum('bhts,bsr->bthr', p.astype(lat.dtype), lat[..., :MLA_KV_LORA])


def mla_out(o_lat, w_uv, w_o):
    B, T = o_lat.shape[:2]
    o = jnp.einsum('bthr,rhv->bthv', o_lat, w_uv)
    return o.reshape(B, T, MLA_HEADS * MLA_V) @ w_o


def mla_prompt(h, w_in, q_norm, kv_norm, w_uq, w_uk, w_uv, w_o):
    T = h.shape[1]
    pos = jnp.arange(T, dtype=jnp.int32)
    lat, q_abs = mla_project(h, pos, w_in, q_norm, kv_norm, w_uq, w_uk)
    o = map_query_blocks(lambda qs, p: latent_attend(qs[0], p, lat, pos), (q_abs,), pos, Q_BLOCK)
    return mla_out(o, w_uv, w_o), lat


def mla_sample(h, pool, page_table, w_in, q_norm, kv_norm, w_uq, w_uk, w_uv, w_o):
    B, T = h.shape[:2]
    past = page_table.shape[1] * PAGE_SIZE
    pos = past + jnp.arange(T, dtype=jnp.int32)
    lat_new, q_abs = mla_project(h, pos, w_in, q_norm, kv_norm, w_uq, w_uk)
    lat = jnp.concatenate([pool[page_table].reshape(B, past, MLA_LAT), lat_new], 1)
    k_pos = jnp.arange(past + T, dtype=jnp.int32)
    o = map_query_blocks(lambda qs, p: latent_attend(qs[0], p, lat, k_pos), (q_abs,), pos, Q_BLOCK)
    return mla_out(o, w_uv, w_o), lat_new


def nsa_project(h, pos, w_in, b_gate):
    B, T = h.shape[:2]
    nq = NSA_HEADS * NSA_HD
    nkv = 6 * NSA_KV_HEADS * NSA_HD
    z = h @ w_in
    q = z[..., :nq].reshape(B, T, NSA_KV_HEADS, NSA_GROUP, NSA_HD)
    kv = z[..., nq:nq + nkv].reshape(B, T, 3, 2, NSA_KV_HEADS, NSA_HD)
    gates = jax.nn.sigmoid(z[..., nq + nkv:] + b_gate).reshape(B, T, NSA_KV_HEADS, NSA_GROUP, 3)
    q_rot = rope(q, pos)

    def rot_k(r):
        return jnp.stack([rope(r[:, :, 0], pos), r[:, :, 1]], axis=2)

    return q, q_rot, gates, kv[:, :, 0], rot_k(kv[:, :, 1]), rot_k(kv[:, :, 2])


def nsa_compress(rows, w1, b1, w2):
    B, T = rows.shape[:2]
    nch = -(-T // NSA_CMP_STRIDE)
    rows = jnp.pad(rows, ((0, 0), (0, nch * NSA_CMP_STRIDE - T), (0, 0), (0, 0)))
    ch = rows.reshape(B, nch, NSA_CMP_STRIDE, NSA_KV_HEADS, NSA_HD)
    nb = nch - NSA_CMP_RATIO + 1
    pre = b1
    for r in range(NSA_CMP_RATIO):
        w1r = w1[r * NSA_CMP_STRIDE:(r + 1) * NSA_CMP_STRIDE]
        pre = pre + jnp.einsum('bclgd,ldh->bcgh', ch[:, r:r + nb], w1r)
    return jnp.einsum('bngh,hd->bngd', jax.nn.gelu(pre), w2)


def nsa_core(q_raw, q_rot, gates, q_pos, kc, vc, gather_slc, n_blocks, kw, vw, kw_pos):
    nb = kc.shape[1]
    cmp_end = jnp.arange(nb, dtype=jnp.int32) * NSA_CMP_STRIDE + NSA_CMP_LEN - 1
    s_c = jnp.einsum('btghd,bngd->btghn', q_raw, kc) * NSA_SCALE
    p_c = masked_softmax(s_c, (cmp_end[None, :] <= q_pos[:, None])[None, :, None, None, :])
    o_c = jnp.einsum('btghn,bngd->btghd', p_c.astype(vc.dtype), vc)
    ci = jnp.arange(nb, dtype=jnp.int32)[:, None] * NSA_CMP_STRIDE
    sj = jnp.arange(n_blocks, dtype=jnp.int32)[None, :] * NSA_SLC_LEN
    cover = ((ci < sj + NSA_SLC_LEN) & (ci + NSA_CMP_LEN > sj)).astype(jnp.float32)
    imp = jnp.einsum('btghn,nj->btgj', p_c, cover)
    j = jnp.arange(n_blocks, dtype=jnp.int32)[None, :]
    cur = (q_pos // NSA_SLC_LEN)[:, None]
    valid = j * NSA_SLC_LEN <= q_pos[:, None]
    forced = (j == 0) | (j == cur) | (j == cur - 1)
    score = jnp.where((valid & forced)[None, :, None, :], NSA_FORCE,
                      jnp.where(valid[None, :, None, :], imp, -jnp.inf))
    _, idx = lax.top_k(score, min(NSA_TOPN, n_blocks))
    sel = gather_slc(idx)
    ks, vs = sel[..., 0, :], sel[..., 1, :]
    k_pos = idx[..., None] * NSA_SLC_LEN + jnp.arange(NSA_SLC_LEN, dtype=jnp.int32)
    s_s = jnp.einsum('btghd,btgnld->btghnl', q_rot, ks) * NSA_SCALE
    B, Tq, G, H, n, L = s_s.shape
    m_s = (k_pos <= q_pos[None, :, None, None, None]).reshape(B, Tq, G, 1, n * L)
    p_s = masked_softmax(s_s.reshape(B, Tq, G, H, n * L), m_s).reshape(s_s.shape)
    o_s = jnp.einsum('btghnl,btgnld->btghd', p_s.astype(vs.dtype), vs)
    s_w = jnp.einsum('btghd,bsgd->btghs', q_rot, kw) * NSA_SCALE
    dpos = q_pos[:, None] - kw_pos[None, :]
    m_w = (dpos >= 0) & (dpos < NSA_WINDOW) & (kw_pos[None, :] >= 0)
    p_w = masked_softmax(s_w, m_w[None, :, None, None, :])
    o_w = jnp.einsum('btghs,bsgd->btghd', p_w.astype(vw.dtype), vw)
    return gates[..., 0:1] * o_c + gates[..., 1:2] * o_s + gates[..., 2:3] * o_w


def _bg(B):
    return (jnp.arange(B)[:, None, None, None], jnp.arange(NSA_KV_HEADS)[None, None, :, None])


def nsa_prompt(h, w_in, b_gate, w1, b1, w2, w_o):
    B, T = h.shape[:2]
    pos = jnp.arange(T, dtype=jnp.int32)
    q, q_rot, gates, cmp_rows, slc_rows, win_rows = nsa_project(h, pos, w_in, b_gate)
    kc = nsa_compress(cmp_rows[:, :, 0], w1[0], b1[0], w2[0])
    vc = nsa_compress(cmp_rows[:, :, 1], w1[1], b1[1], w2[1])
    n_blocks = T // NSA_SLC_LEN
    blocks = slc_rows.reshape(B, n_blocks, NSA_SLC_LEN, 2, NSA_KV_HEADS, NSA_HD)
    bi, gi = _bg(B)

    def gather(idx):
        return blocks[bi, idx, :, :, gi, :]

    kw_pad = jnp.pad(win_rows, ((0, 0), (NSA_WINDOW, 0), (0, 0), (0, 0), (0, 0)))

    def blockfn(qs, p):
        blk = p.shape[0]
        kwb = lax.dynamic_slice_in_dim(kw_pad, p[0], NSA_WINDOW + blk, axis=1)
        kwp = p[0] - NSA_WINDOW + jnp.arange(NSA_WINDOW + blk, dtype=jnp.int32)
        return nsa_core(qs[0], qs[1], qs[2], p, kc, vc, gather, n_blocks, kwb[:, :, 0], kwb[:, :, 1], kwp)

    o = map_query_blocks(blockfn, (q, q_rot, gates), pos, NSA_Q_BLOCK)
    y = o.reshape(B, T, NSA_HEADS * NSA_HD) @ w_o
    wb = min(NSA_WINDOW, T)
    return y, cmp_rows, slc_rows, win_rows[:, T - wb:]


def nsa_sample(h, pool_cmp, pool_slc, win_state, page_table, w_in, b_gate, w1, b1, w2, w_o):
    B, T = h.shape[:2]
    past = page_table.shape[1] * PAGE_SIZE
    pos = past + jnp.arange(T, dtype=jnp.int32)
    q, q_rot, gates, cmp_new, slc_new, win_new = nsa_project(h, pos, w_in, b_gate)
    cmp_all = jnp.concatenate([pool_cmp[page_table].reshape(B, past, 2, NSA_KV_HEADS, NSA_HD), cmp_new], 1)
    kc = nsa_compress(cmp_all[:, :, 0], w1[0], b1[0], w2[0])
    vc = nsa_compress(cmp_all[:, :, 1], w1[1], b1[1], w2[1])
    spp = PAGE_SIZE // NSA_SLC_LEN
    pool_blocks = pool_slc.reshape(pool_slc.shape[0] * spp, NSA_SLC_LEN, 2, NSA_KV_HEADS, NSA_HD)
    ns_past = past // NSA_SLC_LEN
    n_new = -(-T // NSA_SLC_LEN)
    new_blocks = jnp.pad(slc_new, ((0, 0), (0, n_new * NSA_SLC_LEN - T), (0, 0), (0, 0), (0, 0)))
    new_blocks = new_blocks.reshape(B, n_new, NSA_SLC_LEN, 2, NSA_KV_HEADS, NSA_HD)
    bi, gi = _bg(B)

    def gather(idx):
        jp = jnp.minimum(idx, ns_past - 1)
        phys = page_table[bi, jp // spp] * spp + jp % spp
        from_past = pool_blocks[phys, :, :, gi, :]
        jn = jnp.clip(idx - ns_past, 0, n_new - 1)
        from_new = new_blocks[bi, jn, :, :, gi, :]
        return jnp.where((idx < ns_past)[..., None, None, None], from_past, from_new)

    wb = win_state.shape[1]
    win_all = jnp.concatenate([win_state, win_new], 1)
    kw_pos = past - wb + jnp.arange(wb + T, dtype=jnp.int32)
    o = map_query_blocks(
        lambda qs, p: nsa_core(qs[0], qs[1], qs[2], p, kc, vc, gather, ns_past + n_new,
                               win_all[:, :, 0], win_all[:, :, 1], kw_pos),
        (q, q_rot, gates), pos, NSA_Q_BLOCK)
    y = o.reshape(B, T, NSA_HEADS * NSA_HD) @ w_o
    return y, cmp_new, slc_new, win_all[:, T:]


def peer(h, w_q, sub_keys, u, v):
    B, T, D = h.shape
    n = B * T
    pad = (-n) % PEER_CHUNK
    chunks = jnp.pad(h.reshape(n, D), ((0, pad), (0, 0))).reshape(-1, PEER_CHUNK, D)

    def one(xc):
        q = (xc @ w_q).reshape(PEER_CHUNK, PEER_HEADS, 2, PEER_DKEY // 2)
        s = jnp.einsum('chpk,pnk->chpn', q, sub_keys)
        s1, i1 = lax.top_k(s[:, :, 0], PEER_TOPK)
        s2, i2 = lax.top_k(s[:, :, 1], PEER_TOPK)
        cand = (s1[..., :, None] + s2[..., None, :]).reshape(PEER_CHUNK, PEER_HEADS, PEER_TOPK * PEER_TOPK)
        cidx = (i1[..., :, None] * PEER_NKEYS + i2[..., None, :]).reshape(PEER_CHUNK, PEER_HEADS, PEER_TOPK * PEER_TOPK)
        top_s, sel = lax.top_k(cand, PEER_TOPK)
        idx = jnp.take_along_axis(cidx, sel, -1)
        g = jax.nn.softmax(top_s.astype(jnp.float32), -1)
        a = jax.nn.gelu(jnp.einsum('cd,chkd->chk', xc, u[idx]).astype(jnp.float32))
        return jnp.einsum('chk,chkd->cd', (g * a).astype(v.dtype), v[idx])

    out = lax.map(one, chunks).reshape(-1, D)[:n]
    return out.reshape(B, T, D)


def kernel(x_prompt, x_sample, cache_mla_latent, cache_nsa_cmp, cache_nsa_slc, state_nsa_win, page_table,
           norm_mix, norm_ffn, norm_final, mla_w_in, mla_q_norm, mla_kv_norm, mla_w_uq, mla_w_uk, mla_w_uv,
           mla_w_o, nsa_w_in, nsa_b_gate, nsa_phi_w1, nsa_phi_b1, nsa_phi_w2, nsa_w_o, peer_w_q, peer_sub_keys,
           peer_u, peer_v):
    depth = norm_mix.shape[0]
    xp, xs = x_prompt, x_sample
    mla_p, mla_s, cmp_p, cmp_s, slc_p, slc_s, win_p, win_s = [], [], [], [], [], [], [], []
    for i in range(depth):
        j = i // N_MIXERS
        hp = rms_norm(xp, norm_mix[i])
        hs = rms_norm(xs, norm_mix[i])
        if i % N_MIXERS == 0:
            w = (mla_w_in[j], mla_q_norm[j], mla_kv_norm[j], mla_w_uq[j], mla_w_uk[j], mla_w_uv[j], mla_w_o[j])
            op, rp = mla_prompt(hp, *w)
            osm, rs = mla_sample(hs, cache_mla_latent[j], page_table, *w)
            mla_p.append(rp)
            mla_s.append(rs)
        else:
            w = (nsa_w_in[j], nsa_b_gate[j], nsa_phi_w1[j], nsa_phi_b1[j], nsa_phi_w2[j], nsa_w_o[j])
            op, cp, sp, wp = nsa_prompt(hp, *w)
            osm, cs, ss, ws = nsa_sample(hs, cache_nsa_cmp[j], cache_nsa_slc[j], state_nsa_win[j], page_table, *w)
            cmp_p.append(cp)
            cmp_s.append(cs)
            slc_p.append(sp)
            slc_s.append(ss)
            win_p.append(wp)
            win_s.append(ws)
        xp = xp + op
        xs = xs + osm
        pw = (peer_w_q[i], peer_sub_keys[i], peer_u[i], peer_v[i])
        xp = xp + peer(rms_norm(xp, norm_ffn[i]), *pw)
        xs = xs + peer(rms_norm(xs, norm_ffn[i]), *pw)
    y_prompt = rms_norm_pallas(xp, norm_final)
    y_sample = rms_norm_pallas(xs, norm_final)
    return (y_prompt, y_sample, jnp.stack(mla_p), jnp.stack(mla_s), jnp.stack(cmp_p), jnp.stack(cmp_s),
            jnp.stack(slc_p), jnp.stack(slc_s), jnp.stack(win_p), jnp.stack(win_s))
```

```python
import functools

import jax
import jax.numpy as jnp
import numpy as np
from jax import lax
from jax.experimental import pallas as pl
from jax.experimental.pallas import tpu as pltpu

D_MODEL = 1024
PAGE_SIZE = 128
N_MIXERS = 2
NORM_EPS = 1e-6
ROPE_THETA = 10000.0
Q_BLOCK = 128
MLA_HEADS = 8
MLA_Q_LORA = 512
MLA_KV_LORA = 256
MLA_NOPE = 128
MLA_ROPE = 64
MLA_V = 128
MLA_LAT = MLA_KV_LORA + MLA_ROPE
MLA_SCALE = (MLA_NOPE + MLA_ROPE) ** -0.5
NSA_HEADS = 16
NSA_KV_HEADS = 4
NSA_GROUP = NSA_HEADS // NSA_KV_HEADS
NSA_HD = 64
NSA_CMP_LEN = 32
NSA_CMP_STRIDE = 16
NSA_CMP_RATIO = NSA_CMP_LEN // NSA_CMP_STRIDE
NSA_SLC_LEN = 64
NSA_TOPN = 16
NSA_WINDOW = 512
NSA_Q_BLOCK = 64
NSA_SCALE = NSA_HD ** -0.5
NSA_FORCE = 1e9
PEER_HEADS = 8
PEER_NKEYS = 128
PEER_DKEY = 256
PEER_TOPK = 16
PEER_ROUTE_CHUNK = 1024


def _rms_norm_body(x_ref, g_ref, o_ref):
    x = x_ref[...]
    y = x * lax.rsqrt(jnp.mean(x * x, -1, keepdims=True) + NORM_EPS)
    o_ref[...] = y * g_ref[...]


def rms_norm_pallas(x, g, rows=512):
    shp = x.shape
    x2 = x.reshape(-1, shp[-1])
    n, d = x2.shape
    rows = min(rows, n)
    out = pl.pallas_call(
        _rms_norm_body,
        grid=(n // rows,),
        in_specs=[pl.BlockSpec((rows, d), lambda i: (i, 0)),
                  pl.BlockSpec((1, d), lambda i: (0, 0))],
        out_specs=pl.BlockSpec((rows, d), lambda i: (i, 0)),
        out_shape=jax.ShapeDtypeStruct((n, d), x.dtype),
    )(x2, g.reshape(1, d))
    return out.reshape(shp)


LANES = 128
PEER_TOKENS_PER_STEP = 16
PEER_PICKS = PEER_HEADS * PEER_TOPK
PEER_ISSUE_UNROLL = 8
PEER_VMEM_LIMIT_BYTES = 48 * 1024 * 1024


def _peer_expert_body(idx_cur, idx_nxt, x_ref, g_ref, u_hbm, v_hbm, o_ref, ubuf, vbuf, sem):
    tb, d = x_ref.shape
    picks = g_ref.shape[1]
    rows = tb * picks
    nchunk = d // LANES
    i = pl.program_id(0)
    n = pl.num_programs(0)
    slot = lax.rem(i, 2)

    def row_copies(e, s, r):
        src = pl.ds(pl.multiple_of(e * nchunk, nchunk), nchunk)
        dst = pl.ds(pl.multiple_of(r * nchunk, nchunk), nchunk)
        return (pltpu.make_async_copy(u_hbm.at[src], ubuf.at[s, dst], sem.at[0, s]),
                pltpu.make_async_copy(v_hbm.at[src], vbuf.at[s, dst], sem.at[1, s]))

    def issue(idx_ref, s):
        def pick(r, c):
            cu, cv = row_copies(idx_ref[0, r], s, r)
            cu.start()
            cv.start(priority=1)
            return c
        lax.fori_loop(0, rows, pick, 0, unroll=PEER_ISSUE_UNROLL)

    @pl.when(i == 0)
    def _():
        issue(idx_cur, 0)

    @pl.when(i + 1 < n)
    def _():
        issue(idx_nxt, 1 - slot)

    pltpu.make_async_copy(u_hbm.at[pl.ds(0, rows * nchunk)], ubuf.at[slot], sem.at[0, slot]).wait()
    pltpu.make_async_copy(v_hbm.at[pl.ds(0, rows * nchunk)], vbuf.at[slot], sem.at[1, slot]).wait()

    xb = x_ref[...].astype(jnp.bfloat16)
    g = g_ref[...]
    row_id = lax.broadcasted_iota(jnp.int32, (tb, picks), 0)
    acc = [jnp.zeros((tb, LANES), jnp.float32) for _ in range(nchunk)]
    for t in range(tb):
        base = t * picks * nchunk
        a = jnp.zeros((tb, picks), jnp.float32)
        for c in range(nchunk):
            uc = ubuf[slot, pl.ds(base + c, picks, stride=nchunk), :].astype(jnp.bfloat16)
            a = a + lax.dot_general(xb[:, c * LANES:(c + 1) * LANES], uc, (((1,), (1,)), ((), ())),
                                    preferred_element_type=jnp.float32)
        w = jnp.where(row_id == t, g * jax.nn.gelu(a), 0.0).astype(jnp.bfloat16)
        for c in range(nchunk):
            vc = vbuf[slot, pl.ds(base + c, picks, stride=nchunk), :].astype(jnp.bfloat16)
            acc[c] = acc[c] + jnp.dot(w, vc, preferred_element_type=jnp.float32)
    o_ref[...] = jnp.concatenate(acc, axis=1)


def peer_experts_pallas(x, idx, g, u, v):
    n, d = x.shape
    picks = idx.shape[1]
    tb = PEER_TOKENS_PER_STEP
    assert n % tb == 0 and u.shape == v.shape and u.shape[1] == d and d % LANES == 0
    steps = n // tb
    rows = tb * picks
    nchunk = d // LANES
    idx3 = idx.reshape(steps, 1, rows)
    u2 = u.reshape(u.shape[0] * nchunk, LANES)
    v2 = v.reshape(v.shape[0] * nchunk, LANES)
    return pl.pallas_call(
        _peer_expert_body,
        grid=(steps,),
        in_specs=[
            pl.BlockSpec((None, 1, rows), lambda i: (i, 0, 0), memory_space=pltpu.SMEM),
            pl.BlockSpec((None, 1, rows), lambda i: (jnp.minimum(i + 1, steps - 1), 0, 0), memory_space=pltpu.SMEM),
            pl.BlockSpec((tb, d), lambda i: (i, 0)),
            pl.BlockSpec((tb, picks), lambda i: (i, 0)),
            pl.BlockSpec(memory_space=pl.ANY),
            pl.BlockSpec(memory_space=pl.ANY),
        ],
        out_specs=pl.BlockSpec((tb, d), lambda i: (i, 0)),
        out_shape=jax.ShapeDtypeStruct((n, d), jnp.float32),
        scratch_shapes=[
            pltpu.VMEM((2, rows * nchunk, LANES), jnp.float32),
            pltpu.VMEM((2, rows * nchunk, LANES), jnp.float32),
            pltpu.SemaphoreType.DMA((2, 2)),
        ],
        compiler_params=pltpu.CompilerParams(
            dimension_semantics=("arbitrary",),
            vmem_limit_bytes=PEER_VMEM_LIMIT_BYTES),
    )(idx3, idx3, x, g, u2, v2)


def rms_norm(x, g):
    xf = x.astype(jnp.float32)
    y = xf * lax.rsqrt(jnp.mean(xf * xf, -1, keepdims=True) + NORM_EPS)
    return (y * g.astype(jnp.float32)).astype(x.dtype)


def rope(x, pos):
    half = x.shape[-1] // 2
    inv = ROPE_THETA ** (-jnp.arange(half, dtype=jnp.float32) / half)
    ang = pos.astype(jnp.float32)[:, None] * inv[None, :]
    shp = (pos.shape[0],) + (1,) * (x.ndim - 3) + (half,)
    cos, sin = jnp.cos(ang).reshape(shp), jnp.sin(ang).reshape(shp)
    xf = x.astype(jnp.float32)
    x1, x2 = xf[..., :half], xf[..., half:]
    return jnp.concatenate([x1 * cos - x2 * sin, x2 * cos + x1 * sin], -1).astype(x.dtype)


def masked_softmax(s, mask):
    s = jnp.where(mask, s.astype(jnp.float32), -jnp.inf)
    m = jnp.max(s, -1, keepdims=True)
    m = jnp.where(jnp.isfinite(m), m, 0.0)
    p = jnp.exp(s - m)
    den = jnp.sum(p, -1, keepdims=True)
    return p / jnp.where(den > 0, den, 1.0)


def map_query_blocks(fn, qs, q_pos, blk):
    T = q_pos.shape[0]
    if T <= blk:
        return fn(qs, q_pos)
    nb = -(-T // blk)
    pad = nb * blk - T

    def split(a):
        a = jnp.pad(a, [(0, 0), (0, pad)] + [(0, 0)] * (a.ndim - 2))
        return jnp.moveaxis(a.reshape(a.shape[0], nb, blk, *a.shape[2:]), 1, 0)

    qb = tuple(split(a) for a in qs)
    pb = jnp.pad(q_pos, (0, pad), mode='edge').reshape(nb, blk)
    out = lax.map(lambda a: fn(a[0], a[1]), (qb, pb))
    out = jnp.moveaxis(out, 0, 1)
    return out.reshape(out.shape[0], nb * blk, *out.shape[3:])[:, :T]


def mla_project(h, pos, w_in, q_norm, kv_norm, w_uq, w_uk):
    z = h @ w_in
    cq = rms_norm(z[..., :MLA_Q_LORA], q_norm)
    ckv = rms_norm(z[..., MLA_Q_LORA:MLA_Q_LORA + MLA_KV_LORA], kv_norm)
    kr = rope(z[..., MLA_Q_LORA + MLA_KV_LORA:][:, :, None, :], pos)[:, :, 0]
    lat = jnp.concatenate([ckv, kr], -1)
    q = jnp.einsum('btr,rhd->bthd', cq, w_uq)
    q_lat = jnp.einsum('bthn,rhn->bthr', q[..., :MLA_NOPE], w_uk)
    q_abs = jnp.concatenate([q_lat, rope(q[..., MLA_NOPE:], pos)], -1)
    return lat, q_abs


def latent_attend(q_abs, q_pos, lat, k_pos):
    s = jnp.einsum('bthc,bsc->bhts', q_abs, lat) * MLA_SCALE
    p = masked_softmax(s, (k_pos[None, :] <= q_pos[:, None])[None, None])
    return jnp.einsum('bhts,bsr->bthr', p.astype(lat.dtype), lat[..., :MLA_KV_LORA])


def mla_out(o_lat, w_uv, w_o):
    B, T = o_lat.shape[:2]
    o = jnp.einsum('bthr,rhv->bthv', o_lat, w_uv)
    return o.reshape(B, T, MLA_HEADS * MLA_V) @ w_o


def mla_prompt(h, w_in, q_norm, kv_norm, w_uq, w_uk, w_uv, w_o):
    T = h.shape[1]
    pos = jnp.arange(T, dtype=jnp.int32)
    lat, q_abs = mla_project(h, pos, w_in, q_norm, kv_norm, w_uq, w_uk)
    o = map_query_blocks(lambda qs, p: latent_attend(qs[0], p, lat, pos), (q_abs,), pos, Q_BLOCK)
    return mla_out(o, w_uv, w_o), lat


def mla_sample(h, pool, page_table, w_in, q_norm, kv_norm, w_uq, w_uk, w_uv, w_o):
    B, T = h.shape[:2]
    past = page_table.shape[1] * PAGE_SIZE
    pos = past + jnp.arange(T, dtype=jnp.int32)
    lat_new, q_abs = mla_project(h, pos, w_in, q_norm, kv_norm, w_uq, w_uk)
    lat = jnp.concatenate([pool[page_table].reshape(B, past, MLA_LAT), lat_new], 1)
    k_pos = jnp.arange(past + T, dtype=jnp.int32)
    o = map_query_blocks(lambda qs, p: latent_attend(qs[0], p, lat, k_pos), (q_abs,), pos, Q_BLOCK)
    return mla_out(o, w_uv, w_o), lat_new


def nsa_project(h, pos, w_in, b_gate):
    B, T = h.shape[:2]
    nq = NSA_HEADS * NSA_HD
    nkv = 6 * NSA_KV_HEADS * NSA_HD
    z = h @ w_in
    q = z[..., :nq].reshape(B, T, NSA_KV_HEADS, NSA_GROUP, NSA_HD)
    kv = z[..., nq:nq + nkv].reshape(B, T, 3, 2, NSA_KV_HEADS, NSA_HD)
    gates = jax.nn.sigmoid(z[..., nq + nkv:] + b_gate).reshape(B, T, NSA_KV_HEADS, NSA_GROUP, 3)
    q_rot = rope(q, pos)

    def rot_k(r):
        return jnp.stack([rope(r[:, :, 0], pos), r[:, :, 1]], axis=2)

    return q, q_rot, gates, kv[:, :, 0], rot_k(kv[:, :, 1]), rot_k(kv[:, :, 2])


def nsa_compress(rows, w1, b1, w2):
    B, T = rows.shape[:2]
    nch = -(-T // NSA_CMP_STRIDE)
    rows = jnp.pad(rows, ((0, 0), (0, nch * NSA_CMP_STRIDE - T), (0, 0), (0, 0)))
    ch = rows.reshape(B, nch, NSA_CMP_STRIDE, NSA_KV_HEADS, NSA_HD)
    nb = nch - NSA_CMP_RATIO + 1
    pre = b1
    for r in range(NSA_CMP_RATIO):
        w1r = w1[r * NSA_CMP_STRIDE:(r + 1) * NSA_CMP_STRIDE]
        pre = pre + jnp.einsum('bclgd,ldh->bcgh', ch[:, r:r + nb], w1r)
    return jnp.einsum('bngh,hd->bngd', jax.nn.gelu(pre), w2)


def nsa_core(q_raw, q_rot, gates, q_pos, kc, vc, gather_slc, n_blocks, kw, vw, kw_pos):
    nb = kc.shape[1]
    cmp_end = jnp.arange(nb, dtype=jnp.int32) * NSA_CMP_STRIDE + NSA_CMP_LEN - 1
    s_c = jnp.einsum('btghd,bngd->btghn', q_raw, kc) * NSA_SCALE
    p_c = masked_softmax(s_c, (cmp_end[None, :] <= q_pos[:, None])[None, :, None, None, :])
    o_c = jnp.einsum('btghn,bngd->btghd', p_c.astype(vc.dtype), vc)
    ci = jnp.arange(nb, dtype=jnp.int32)[:, None] * NSA_CMP_STRIDE
    sj = jnp.arange(n_blocks, dtype=jnp.int32)[None, :] * NSA_SLC_LEN
    cover = ((ci < sj + NSA_SLC_LEN) & (ci + NSA_CMP_LEN > sj)).astype(jnp.float32)
    imp = jnp.einsum('btghn,nj->btgj', p_c, cover)
    j = jnp.arange(n_blocks, dtype=jnp.int32)[None, :]
    cur = (q_pos // NSA_SLC_LEN)[:, None]
    valid = j * NSA_SLC_LEN <= q_pos[:, None]
    forced = (j == 0) | (j == cur) | (j == cur - 1)
    score = jnp.where((valid & forced)[None, :, None, :], NSA_FORCE,
                      jnp.where(valid[None, :, None, :], imp, -jnp.inf))
    _, idx = lax.top_k(score, min(NSA_TOPN, n_blocks))
    sel = gather_slc(idx)
    ks, vs = sel[..., 0, :], sel[..., 1, :]
    k_pos = idx[..., None] * NSA_SLC_LEN + jnp.arange(NSA_SLC_LEN, dtype=jnp.int32)
    s_s = jnp.einsum('btghd,btgnld->btghnl', q_rot, ks) * NSA_SCALE
    B, Tq, G, H, n, L = s_s.shape
    m_s = (k_pos <= q_pos[None, :, None, None, None]).reshape(B, Tq, G, 1, n * L)
    p_s = masked_softmax(s_s.reshape(B, Tq, G, H, n * L), m_s).reshape(s_s.shape)
    o_s = jnp.einsum('btghnl,btgnld->btghd', p_s.astype(vs.dtype), vs)
    s_w = jnp.einsum('btghd,bsgd->btghs', q_rot, kw) * NSA_SCALE
    dpos = q_pos[:, None] - kw_pos[None, :]
    m_w = (dpos >= 0) & (dpos < NSA_WINDOW) & (kw_pos[None, :] >= 0)
    p_w = masked_softmax(s_w, m_w[None, :, None, None, :])
    o_w = jnp.einsum('btghs,bsgd->btghd', p_w.astype(vw.dtype), vw)
    return gates[..., 0:1] * o_c + gates[..., 1:2] * o_s + gates[..., 2:3] * o_w


def _bg(B):
    return (jnp.arange(B)[:, None, None, None], jnp.arange(NSA_KV_HEADS)[None, None, :, None])


def nsa_prompt(h, w_in, b_gate, w1, b1, w2, w_o):
    B, T = h.shape[:2]
    pos = jnp.arange(T, dtype=jnp.int32)
    q, q_rot, gates, cmp_rows, slc_rows, win_rows = nsa_project(h, pos, w_in, b_gate)
    kc = nsa_compress(cmp_rows[:, :, 0], w1[0], b1[0], w2[0])
    vc = nsa_compress(cmp_rows[:, :, 1], w1[1], b1[1], w2[1])
    n_blocks = T // NSA_SLC_LEN
    blocks = slc_rows.reshape(B, n_blocks, NSA_SLC_LEN, 2, NSA_KV_HEADS, NSA_HD)
    bi, gi = _bg(B)

    def gather(idx):
        return blocks[bi, idx, :, :, gi, :]

    kw_pad = jnp.pad(win_rows, ((0, 0), (NSA_WINDOW, 0), (0, 0), (0, 0), (0, 0)))

    def blockfn(qs, p):
        blk = p.shape[0]
        kwb = lax.dynamic_slice_in_dim(kw_pad, p[0], NSA_WINDOW + blk, axis=1)
        kwp = p[0] - NSA_WINDOW + jnp.arange(NSA_WINDOW + blk, dtype=jnp.int32)
        return nsa_core(qs[0], qs[1], qs[2], p, kc, vc, gather, n_blocks, kwb[:, :, 0], kwb[:, :, 1], kwp)

    o = map_query_blocks(blockfn, (q, q_rot, gates), pos, NSA_Q_BLOCK)
    y = o.reshape(B, T, NSA_HEADS * NSA_HD) @ w_o
    wb = min(NSA_WINDOW, T)
    return y, cmp_rows, slc_rows, win_rows[:, T - wb:]


def nsa_sample(h, pool_cmp, pool_slc, win_state, page_table, w_in, b_gate, w1, b1, w2, w_o):
    B, T = h.shape[:2]
    past = page_table.shape[1] * PAGE_SIZE
    pos = past + jnp.arange(T, dtype=jnp.int32)
    q, q_rot, gates, cmp_new, slc_new, win_new = nsa_project(h, pos, w_in, b_gate)
    cmp_all = jnp.concatenate([pool_cmp[page_table].reshape(B, past, 2, NSA_KV_HEADS, NSA_HD), cmp_new], 1)
    kc = nsa_compress(cmp_all[:, :, 0], w1[0], b1[0], w2[0])
    vc = nsa_compress(cmp_all[:, :, 1], w1[1], b1[1], w2[1])
    spp = PAGE_SIZE // NSA_SLC_LEN
    pool_blocks = pool_slc.reshape(pool_slc.shape[0] * spp, NSA_SLC_LEN, 2, NSA_KV_HEADS, NSA_HD)
    ns_past = past // NSA_SLC_LEN
    n_new = -(-T // NSA_SLC_LEN)
    new_blocks = jnp.pad(slc_new, ((0, 0), (0, n_new * NSA_SLC_LEN - T), (0, 0), (0, 0), (0, 0)))
    new_blocks = new_blocks.reshape(B, n_new, NSA_SLC_LEN, 2, NSA_KV_HEADS, NSA_HD)
    bi, gi = _bg(B)

    def gather(idx):
        jp = jnp.minimum(idx, ns_past - 1)
        phys = page_table[bi, jp // spp] * spp + jp % spp
        from_past = pool_blocks[phys, :, :, gi, :]
        jn = jnp.clip(idx - ns_past, 0, n_new - 1)
        from_new = new_blocks[bi, jn, :, :, gi, :]
        return jnp.where((idx < ns_past)[..., None, None, None], from_past, from_new)

    wb = win_state.shape[1]
    win_all = jnp.concatenate([win_state, win_new], 1)
    kw_pos = past - wb + jnp.arange(wb + T, dtype=jnp.int32)
    o = map_query_blocks(
        lambda qs, p: nsa_core(qs[0], qs[1], qs[2], p, kc, vc, gather, ns_past + n_new,
                               win_all[:, :, 0], win_all[:, :, 1], kw_pos),
        (q, q_rot, gates), pos, NSA_Q_BLOCK)
    y = o.reshape(B, T, NSA_HEADS * NSA_HD) @ w_o
    return y, cmp_new, slc_new, win_all[:, T:]


def peer_route(xc, w_q, sub_keys):
    c = xc.shape[0]
    q = (xc @ w_q).reshape(c, PEER_HEADS, 2, PEER_DKEY // 2)
    s = jnp.einsum('chpk,pnk->chpn', q, sub_keys)
    s1, i1 = lax.top_k(s[:, :, 0], PEER_TOPK)
    s2, i2 = lax.top_k(s[:, :, 1], PEER_TOPK)
    cand = (s1[..., :, None] + s2[..., None, :]).reshape(c, PEER_HEADS, PEER_TOPK * PEER_TOPK)
    cidx = (i1[..., :, None] * PEER_NKEYS + i2[..., None, :]).reshape(c, PEER_HEADS, PEER_TOPK * PEER_TOPK)
    top_s, sel = lax.top_k(cand, PEER_TOPK)
    idx = jnp.take_along_axis(cidx, sel, -1)
    g = jax.nn.softmax(top_s.astype(jnp.float32), -1)
    return idx.reshape(c, PEER_PICKS).astype(jnp.int32), g.reshape(c, PEER_PICKS)


def peer(h, w_q, sub_keys, u, v):
    B, T, D = h.shape
    n = B * T
    x = h.reshape(n, D)
    chunk = min(n, PEER_ROUTE_CHUNK)
    idx, g = lax.map(lambda xc: peer_route(xc, w_q, sub_keys), x.reshape(n // chunk, chunk, D))
    out = peer_experts_pallas(x, idx.reshape(n, PEER_PICKS), g.reshape(n, PEER_PICKS), u, v)
    return out.reshape(B, T, D)


def kernel(x_prompt, x_sample, cache_mla_latent, cache_nsa_cmp, cache_nsa_slc, state_nsa_win, page_table,
           norm_mix, norm_ffn, norm_final, mla_w_in, mla_q_norm, mla_kv_norm, mla_w_uq, mla_w_uk, mla_w_uv,
           mla_w_o, nsa_w_in, nsa_b_gate, nsa_phi_w1, nsa_phi_b1, nsa_phi_w2, nsa_w_o, peer_w_q, peer_sub_keys,
           peer_u, peer_v):
    depth = norm_mix.shape[0]
    xp, xs = x_prompt, x_sample
    mla_p, mla_s, cmp_p, cmp_s, slc_p, slc_s, win_p, win_s = [], [], [], [], [], [], [], []
    for i in range(depth):
        j = i // N_MIXERS
        hp = rms_norm(xp, norm_mix[i])
        hs = rms_norm(xs, norm_mix[i])
        if i % N_MIXERS == 0:
            w = (mla_w_in[j], mla_q_norm[j], mla_kv_norm[j], mla_w_uq[j], mla_w_uk[j], mla_w_uv[j], mla_w_o[j])
            op, rp = mla_prompt(hp, *w)
            osm, rs = mla_sample(hs, cache_mla_latent[j], page_table, *w)
            mla_p.append(rp)
            mla_s.append(rs)
        else:
            w = (nsa_w_in[j], nsa_b_gate[j], nsa_phi_w1[j], nsa_phi_b1[j], nsa_phi_w2[j], nsa_w_o[j])
            op, cp, sp, wp = nsa_prompt(hp, *w)
            osm, cs, ss, ws = nsa_sample(hs, cache_nsa_cmp[j], cache_nsa_slc[j], state_nsa_win[j], page_table, *w)
            cmp_p.append(cp)
            cmp_s.append(cs)
            slc_p.append(sp)
            slc_s.append(ss)
            win_p.append(wp)
            win_s.append(ws)
        xp = xp + op
        xs = xs + osm
        pw = (peer_w_q[i], peer_sub_keys[i], peer_u[i], peer_v[i])
        xp = xp + peer(rms_norm(xp, norm_ffn[i]), *pw)
        xs = xs + peer(rms_norm(xs, norm_ffn[i]), *pw)
    y_prompt = rms_norm_pallas(xp, norm_final)
    y_sample = rms_norm_pallas(xs, norm_final)
    return (y_prompt, y_sample, jnp.stack(mla_p), jnp.stack(mla_s), jnp.stack(cmp_p), jnp.stack(cmp_s),
            jnp.stack(slc_p), jnp.stack(slc_s), jnp.stack(win_p), jnp.stack(win_s))
```

```python
import functools

import jax
import jax.numpy as jnp
import numpy as np
from jax import lax
from jax.experimental import pallas as pl
from jax.experimental.pallas import tpu as pltpu

D_MODEL = 1024
PAGE_SIZE = 128
N_MIXERS = 2
NORM_EPS = 1e-6
ROPE_THETA = 10000.0
Q_BLOCK = 128
MLA_HEADS = 8
MLA_Q_LORA = 512
MLA_KV_LORA = 256
MLA_NOPE = 128
MLA_ROPE = 64
MLA_V = 128
MLA_LAT = MLA_KV_LORA + MLA_ROPE
MLA_SCALE = (MLA_NOPE + MLA_ROPE) ** -0.5
NSA_HEADS = 16
NSA_KV_HEADS = 4
NSA_GROUP = NSA_HEADS // NSA_KV_HEADS
NSA_HD = 64
NSA_CMP_LEN = 32
NSA_CMP_STRIDE = 16
NSA_CMP_RATIO = NSA_CMP_LEN // NSA_CMP_STRIDE
NSA_SLC_LEN = 64
NSA_TOPN = 16
NSA_WINDOW = 512
NSA_Q_BLOCK = 64
NSA_SCALE = NSA_HD ** -0.5
NSA_FORCE = 1e9
PEER_HEADS = 8
PEER_NKEYS = 128
PEER_DKEY = 256
PEER_TOPK = 16
PEER_ROUTE_CHUNK = 1024


def _rms_norm_body(x_ref, g_ref, o_ref):
    x = x_ref[...]
    y = x * lax.rsqrt(jnp.mean(x * x, -1, keepdims=True) + NORM_EPS)
    o_ref[...] = y * g_ref[...]


def rms_norm_pallas(x, g, rows=512):
    shp = x.shape
    x2 = x.reshape(-1, shp[-1])
    n, d = x2.shape
    rows = min(rows, n)
    out = pl.pallas_call(
        _rms_norm_body,
        grid=(n // rows,),
        in_specs=[pl.BlockSpec((rows, d), lambda i: (i, 0)),
                  pl.BlockSpec((1, d), lambda i: (0, 0))],
        out_specs=pl.BlockSpec((rows, d), lambda i: (i, 0)),
        out_shape=jax.ShapeDtypeStruct((n, d), x.dtype),
    )(x2, g.reshape(1, d))
    return out.reshape(shp)


LANES = 128
PEER_TOKENS_PER_STEP = 16
PEER_PICKS = PEER_HEADS * PEER_TOPK
PEER_ISSUE_UNROLL = 8
PEER_VMEM_LIMIT_BYTES = 48 * 1024 * 1024


def _peer_expert_body(idx_cur, idx_nxt, x_ref, g_ref, u_hbm, v_hbm, o_ref, ubuf, vbuf, sem):
    tb, d = x_ref.shape
    picks = g_ref.shape[1]
    rows = tb * picks
    nchunk = d // LANES
    i = pl.program_id(0)
    n = pl.num_programs(0)
    slot = lax.rem(i, 2)

    def row_copies(e, s, r):
        src = pl.ds(pl.multiple_of(e * nchunk, nchunk), nchunk)
        dst = pl.ds(pl.multiple_of(r * nchunk, nchunk), nchunk)
        return (pltpu.make_async_copy(u_hbm.at[src], ubuf.at[s, dst], sem.at[0, s]),
                pltpu.make_async_copy(v_hbm.at[src], vbuf.at[s, dst], sem.at[1, s]))

    def issue(idx_ref, s):
        def pick(r, c):
            cu, cv = row_copies(idx_ref[0, r], s, r)
            cu.start()
            cv.start(priority=1)
            return c
        lax.fori_loop(0, rows, pick, 0, unroll=PEER_ISSUE_UNROLL)

    @pl.when(i == 0)
    def _():
        issue(idx_cur, 0)

    @pl.when(i + 1 < n)
    def _():
        issue(idx_nxt, 1 - slot)

    pltpu.make_async_copy(u_hbm.at[pl.ds(0, rows * nchunk)], ubuf.at[slot], sem.at[0, slot]).wait()
    pltpu.make_async_copy(v_hbm.at[pl.ds(0, rows * nchunk)], vbuf.at[slot], sem.at[1, slot]).wait()

    xb = x_ref[...].astype(jnp.bfloat16)
    g = g_ref[...]
    row_id = lax.broadcasted_iota(jnp.int32, (tb, picks), 0)

    def picked_rows(buf, t):
        base = t * picks * nchunk
        return jnp.concatenate(
            [buf[slot, pl.ds(base + c, picks, stride=nchunk), :].astype(jnp.bfloat16) for c in range(nchunk)], axis=1)

    acc = jnp.zeros((tb, d), jnp.float32)
    for t in range(tb):
        a = lax.dot_general(xb, picked_rows(ubuf, t), (((1,), (1,)), ((), ())), preferred_element_type=jnp.float32)
        w = jnp.where(row_id == t, g * jax.nn.gelu(a), 0.0).astype(jnp.bfloat16)
        acc = acc + jnp.dot(w, picked_rows(vbuf, t), preferred_element_type=jnp.float32)
    o_ref[...] = acc


def peer_experts_pallas(x, idx, g, u, v):
    n, d = x.shape
    picks = idx.shape[1]
    tb = PEER_TOKENS_PER_STEP
    assert n % tb == 0 and u.shape == v.shape and u.shape[1] == d and d % LANES == 0
    steps = n // tb
    rows = tb * picks
    nchunk = d // LANES
    idx3 = idx.reshape(steps, 1, rows)
    u2 = u.reshape(u.shape[0] * nchunk, LANES)
    v2 = v.reshape(v.shape[0] * nchunk, LANES)
    return pl.pallas_call(
        _peer_expert_body,
        grid=(steps,),
        in_specs=[
            pl.BlockSpec((None, 1, rows), lambda i: (i, 0, 0), memory_space=pltpu.SMEM),
            pl.BlockSpec((None, 1, rows), lambda i: (jnp.minimum(i + 1, steps - 1), 0, 0), memory_space=pltpu.SMEM),
            pl.BlockSpec((tb, d), lambda i: (i, 0)),
            pl.BlockSpec((tb, picks), lambda i: (i, 0)),
            pl.BlockSpec(memory_space=pl.ANY),
            pl.BlockSpec(memory_space=pl.ANY),
        ],
        out_specs=pl.BlockSpec((tb, d), lambda i: (i, 0)),
        out_shape=jax.ShapeDtypeStruct((n, d), jnp.float32),
        scratch_shapes=[
            pltpu.VMEM((2, rows * nchunk, LANES), jnp.float32),
            pltpu.VMEM((2, rows * nchunk, LANES), jnp.float32),
            pltpu.SemaphoreType.DMA((2, 2)),
        ],
        compiler_params=pltpu.CompilerParams(
            dimension_semantics=("arbitrary",),
            vmem_limit_bytes=PEER_VMEM_LIMIT_BYTES),
    )(idx3, idx3, x, g, u2, v2)


NSA_TQ = 64
NSA_VMEM_LIMIT_BYTES = 48 * 1024 * 1024


def _masked_softmax_rows(s, mask):
    s = jnp.where(mask, s, -jnp.inf)
    m = jnp.max(s, -1, keepdims=True)
    m = jnp.where(m > -jnp.inf, m, 0.0)
    p = jnp.exp(s - m)
    den = jnp.sum(p, -1, keepdims=True)
    return p * (1.0 / jnp.where(den > 0, den, 1.0))


def _nsa_prompt_body(qraw_ref, qrot_ref, gate_ref, kc_ref, vc_ref, ks_ref, vs_ref, kw_ref, vw_ref,
                     cover_ref, expand_ref, o_ref, *, topn):
    hpg, tq, hd = qraw_ref.shape
    rows = hpg * tq
    seq = ks_ref.shape[0]
    ncmp = kc_ref.shape[0]
    nblk = cover_ref.shape[1]
    q0 = pl.multiple_of(pl.program_id(2) * tq, tq)
    nt = (((1,), (1,)), ((), ()))
    bf16 = jnp.bfloat16

    q_raw = qraw_ref[...].reshape(rows, hd).astype(bf16)
    q_rot = qrot_ref[...].reshape(rows, hd).astype(bf16)
    qpos = q0 + (lax.broadcasted_iota(jnp.int32, (rows, 1), 0) & (tq - 1))

    s_c = lax.dot_general(q_raw, kc_ref[...], nt, preferred_element_type=jnp.float32) * NSA_SCALE
    cmp_end = lax.broadcasted_iota(jnp.int32, (rows, ncmp), 1) * NSA_CMP_STRIDE + (NSA_CMP_LEN - 1)
    p_c = _masked_softmax_rows(s_c, cmp_end <= qpos)
    p_cb = p_c.astype(bf16)
    o_c = jnp.dot(p_cb, vc_ref[...], preferred_element_type=jnp.float32)

    cover = cover_ref[...]
    imp = jnp.dot(p_cb[0:tq], cover, preferred_element_type=jnp.float32)
    for h in range(1, hpg):
        imp = imp + jnp.dot(p_cb[h * tq:(h + 1) * tq], cover, preferred_element_type=jnp.float32)
    j_id = lax.broadcasted_iota(jnp.int32, (tq, nblk), 1)
    tpos = q0 + lax.broadcasted_iota(jnp.int32, (tq, nblk), 0)
    cur = lax.shift_right_logical(tpos, NSA_SLC_LEN.bit_length() - 1)
    valid = j_id * NSA_SLC_LEN <= tpos
    forced = (j_id == 0) | (j_id == cur) | (j_id == cur - 1)
    score = jnp.where(valid & forced, NSA_FORCE, jnp.where(valid, imp, -jnp.inf))
    rank = jnp.zeros((tq, nblk), jnp.int32)
    for jp in range(nblk):
        col = score[:, jp:jp + 1]
        before = (col > score) | ((col == score) & (j_id > jp))
        rank = rank + before.astype(jnp.int32)
    sel = jnp.where(rank < topn, 1.0, 0.0).astype(bf16)
    sel_rows = jnp.concatenate([sel] * hpg, axis=0)
    sel_keys = jnp.dot(sel_rows, expand_ref[...], preferred_element_type=jnp.float32)

    s_s = lax.dot_general(q_rot, ks_ref[...], nt, preferred_element_type=jnp.float32) * NSA_SCALE
    kpos = lax.broadcasted_iota(jnp.int32, (rows, seq), 1)
    p_s = _masked_softmax_rows(s_s, (sel_keys > 0.5) & (kpos <= qpos))
    o_s = jnp.dot(p_s.astype(bf16), vs_ref[...], preferred_element_type=jnp.float32)

    band = NSA_WINDOW + tq
    kw = kw_ref[pl.ds(q0, band), :]
    vw = vw_ref[pl.ds(q0, band), :]
    s_w = lax.dot_general(q_rot, kw, nt, preferred_element_type=jnp.float32) * NSA_SCALE
    kwpos = q0 - NSA_WINDOW + lax.broadcasted_iota(jnp.int32, (rows, band), 1)
    dpos = qpos - kwpos
    p_w = _masked_softmax_rows(s_w, (dpos >= 0) & (dpos < NSA_WINDOW) & (kwpos >= 0))
    o_w = jnp.dot(p_w.astype(bf16), vw, preferred_element_type=jnp.float32)

    gts = gate_ref[...].reshape(rows, 3)
    o = gts[:, 0:1] * o_c + gts[:, 1:2] * o_s + gts[:, 2:3] * o_w
    o_ref[...] = o.reshape(hpg, tq, hd)


def nsa_prompt_attend_pallas(q, q_rot, gates, kc, vc, slc_rows, win_rows):
    B, T, G, H, D = q.shape
    nb = kc.shape[1]
    tq = NSA_TQ
    assert T % tq == 0 and tq & (tq - 1) == 0 and T % NSA_SLC_LEN == 0
    nblk = T // NSA_SLC_LEN
    topn = min(NSA_TOPN, nblk)
    ncmp = -(-nb // LANES) * LANES
    bf16 = jnp.bfloat16
    heads_first = lambda a: jnp.transpose(a, (0, 2, 3, 1, 4))
    kv_first = lambda a: jnp.transpose(a, (0, 2, 1, 3)).astype(bf16)
    pad_cmp = lambda a: jnp.pad(kv_first(a), ((0, 0), (0, 0), (0, ncmp - nb), (0, 0)))
    pad_win = lambda a: jnp.pad(kv_first(a), ((0, 0), (0, 0), (NSA_WINDOW, 0), (0, 0)))
    ci = np.arange(ncmp)[:, None] * NSA_CMP_STRIDE
    sj = np.arange(nblk)[None, :] * NSA_SLC_LEN
    cover = jnp.asarray((ci < sj + NSA_SLC_LEN) & (ci + NSA_CMP_LEN > sj) & (np.arange(ncmp)[:, None] < nb), bf16)
    expand = jnp.asarray(np.arange(nblk)[:, None] == (np.arange(T)[None, :] // NSA_SLC_LEN), bf16)

    qspec = lambda last: pl.BlockSpec((None, None, H, tq, last), lambda b, g, t: (b, g, 0, t, 0))
    kvspec = lambda s: pl.BlockSpec((None, None, s, D), lambda b, g, t: (b, g, 0, 0))
    const = lambda a: pl.BlockSpec(a.shape, lambda b, g, t: (0, 0))
    out = pl.pallas_call(
        functools.partial(_nsa_prompt_body, topn=topn),
        grid=(B, G, T // tq),
        in_specs=[qspec(D), qspec(D), qspec(3), kvspec(ncmp), kvspec(ncmp), kvspec(T), kvspec(T),
                  kvspec(T + NSA_WINDOW), kvspec(T + NSA_WINDOW), const(cover), const(expand)],
        out_specs=qspec(D),
        out_shape=jax.ShapeDtypeStruct((B, G, H, T, D), jnp.float32),
        compiler_params=pltpu.CompilerParams(
            dimension_semantics=("arbitrary", "arbitrary", "arbitrary"),
            vmem_limit_bytes=NSA_VMEM_LIMIT_BYTES),
    )(heads_first(q), heads_first(q_rot), heads_first(gates), pad_cmp(kc), pad_cmp(vc),
      kv_first(slc_rows[:, :, 0]), kv_first(slc_rows[:, :, 1]),
      pad_win(win_rows[:, :, 0]), pad_win(win_rows[:, :, 1]), cover, expand)
    return jnp.transpose(out, (0, 3, 1, 2, 4))


def rms_norm(x, g):
    xf = x.astype(jnp.float32)
    y = xf * lax.rsqrt(jnp.mean(xf * xf, -1, keepdims=True) + NORM_EPS)
    return (y * g.astype(jnp.float32)).astype(x.dtype)


def rope(x, pos):
    half = x.shape[-1] // 2
    inv = ROPE_THETA ** (-jnp.arange(half, dtype=jnp.float32) / half)
    ang = pos.astype(jnp.float32)[:, None] * inv[None, :]
    shp = (pos.shape[0],) + (1,) * (x.ndim - 3) + (half,)
    cos, sin = jnp.cos(ang).reshape(shp), jnp.sin(ang).reshape(shp)
    xf = x.astype(jnp.float32)
    x1, x2 = xf[..., :half], xf[..., half:]
    return jnp.concatenate([x1 * cos - x2 * sin, x2 * cos + x1 * sin], -1).astype(x.dtype)


def masked_softmax(s, mask):
    s = jnp.where(mask, s.astype(jnp.float32), -jnp.inf)
    m = jnp.max(s, -1, keepdims=True)
    m = jnp.where(jnp.isfinite(m), m, 0.0)
    p = jnp.exp(s - m)
    den = jnp.sum(p, -1, keepdims=True)
    return p / jnp.where(den > 0, den, 1.0)


def map_query_blocks(fn, qs, q_pos, blk):
    T = q_pos.shape[0]
    if T <= blk:
        return fn(qs, q_pos)
    nb = -(-T // blk)
    pad = nb * blk - T

    def split(a):
        a = jnp.pad(a, [(0, 0), (0, pad)] + [(0, 0)] * (a.ndim - 2))
        return jnp.moveaxis(a.reshape(a.shape[0], nb, blk, *a.shape[2:]), 1, 0)

    qb = tuple(split(a) for a in qs)
    pb = jnp.pad(q_pos, (0, pad), mode='edge').reshape(nb, blk)
    out = lax.map(lambda a: fn(a[0], a[1]), (qb, pb))
    out = jnp.moveaxis(out, 0, 1)
    return out.reshape(out.shape[0], nb * blk, *out.shape[3:])[:, :T]


def mla_project(h, pos, w_in, q_norm, kv_norm, w_uq, w_uk):
    z = h @ w_in
    cq = rms_norm(z[..., :MLA_Q_LORA], q_norm)
    ckv = rms_norm(z[..., MLA_Q_LORA:MLA_Q_LORA + MLA_KV_LORA], kv_norm)
    kr = rope(z[..., MLA_Q_LORA + MLA_KV_LORA:][:, :, None, :], pos)[:, :, 0]
    lat = jnp.concatenate([ckv, kr], -1)
    q = jnp.einsum('btr,rhd->bthd', cq, w_uq)
    q_lat = jnp.einsum('bthn,rhn->bthr', q[..., :MLA_NOPE], w_uk)
    q_abs = jnp.concatenate([q_lat, rope(q[..., MLA_NOPE:], pos)], -1)
    return lat, q_abs


def latent_attend(q_abs, q_pos, lat, k_pos):
    s = jnp.einsum('bthc,bsc->bhts', q_abs, lat) * MLA_SCALE
    p = masked_softmax(s, (k_pos[None, :] <= q_pos[:, None])[None, None])
    return jnp.einsum('bhts,bsr->bthr', p.astype(lat.dtype), lat[..., :MLA_KV_LORA])


def mla_out(o_lat, w_uv, w_o):
    B, T = o_lat.shape[:2]
    o = jnp.einsum('bthr,rhv->bthv', o_lat, w_uv)
    return o.reshape(B, T, MLA_HEADS * MLA_V) @ w_o


def mla_prompt(h, w_in, q_norm, kv_norm, w_uq, w_uk, w_uv, w_o):
    T = h.shape[1]
    pos = jnp.arange(T, dtype=jnp.int32)
    lat, q_abs = mla_project(h, pos, w_in, q_norm, kv_norm, w_uq, w_uk)
    o = map_query_blocks(lambda qs, p: latent_attend(qs[0], p, lat, pos), (q_abs,), pos, Q_BLOCK)
    return mla_out(o, w_uv, w_o), lat


def mla_sample(h, pool, page_table, w_in, q_norm, kv_norm, w_uq, w_uk, w_uv, w_o):
    B, T = h.shape[:2]
    past = page_table.shape[1] * PAGE_SIZE
    pos = past + jnp.arange(T, dtype=jnp.int32)
    lat_new, q_abs = mla_project(h, pos, w_in, q_norm, kv_norm, w_uq, w_uk)
    lat = jnp.concatenate([pool[page_table].reshape(B, past, MLA_LAT), lat_new], 1)
    k_pos = jnp.arange(past + T, dtype=jnp.int32)
    o = map_query_blocks(lambda qs, p: latent_attend(qs[0], p, lat, k_pos), (q_abs,), pos, Q_BLOCK)
    return mla_out(o, w_uv, w_o), lat_new


def nsa_project(h, pos, w_in, b_gate):
    B, T = h.shape[:2]
    nq = NSA_HEADS * NSA_HD
    nkv = 6 * NSA_KV_HEADS * NSA_HD
    z = h @ w_in
    q = z[..., :nq].reshape(B, T, NSA_KV_HEADS, NSA_GROUP, NSA_HD)
    kv = z[..., nq:nq + nkv].reshape(B, T, 3, 2, NSA_KV_HEADS, NSA_HD)
    gates = jax.nn.sigmoid(z[..., nq + nkv:] + b_gate).reshape(B, T, NSA_KV_HEADS, NSA_GROUP, 3)
    q_rot = rope(q, pos)

    def rot_k(r):
        return jnp.stack([rope(r[:, :, 0], pos), r[:, :, 1]], axis=2)

    return q, q_rot, gates, kv[:, :, 0], rot_k(kv[:, :, 1]), rot_k(kv[:, :, 2])


def nsa_compress(rows, w1, b1, w2):
    B, T = rows.shape[:2]
    nch = -(-T // NSA_CMP_STRIDE)
    rows = jnp.pad(rows, ((0, 0), (0, nch * NSA_CMP_STRIDE - T), (0, 0), (0, 0)))
    ch = rows.reshape(B, nch, NSA_CMP_STRIDE, NSA_KV_HEADS, NSA_HD)
    nb = nch - NSA_CMP_RATIO + 1
    pre = b1
    for r in range(NSA_CMP_RATIO):
        w1r = w1[r * NSA_CMP_STRIDE:(r + 1) * NSA_CMP_STRIDE]
        pre = pre + jnp.einsum('bclgd,ldh->bcgh', ch[:, r:r + nb], w1r)
    return jnp.einsum('bngh,hd->bngd', jax.nn.gelu(pre), w2)


def nsa_core(q_raw, q_rot, gates, q_pos, kc, vc, gather_slc, n_blocks, kw, vw, kw_pos):
    nb = kc.shape[1]
    cmp_end = jnp.arange(nb, dtype=jnp.int32) * NSA_CMP_STRIDE + NSA_CMP_LEN - 1
    s_c = jnp.einsum('btghd,bngd->btghn', q_raw, kc) * NSA_SCALE
    p_c = masked_softmax(s_c, (cmp_end[None, :] <= q_pos[:, None])[None, :, None, None, :])
    o_c = jnp.einsum('btghn,bngd->btghd', p_c.astype(vc.dtype), vc)
    ci = jnp.arange(nb, dtype=jnp.int32)[:, None] * NSA_CMP_STRIDE
    sj = jnp.arange(n_blocks, dtype=jnp.int32)[None, :] * NSA_SLC_LEN
    cover = ((ci < sj + NSA_SLC_LEN) & (ci + NSA_CMP_LEN > sj)).astype(jnp.float32)
    imp = jnp.einsum('btghn,nj->btgj', p_c, cover)
    j = jnp.arange(n_blocks, dtype=jnp.int32)[None, :]
    cur = (q_pos // NSA_SLC_LEN)[:, None]
    valid = j * NSA_SLC_LEN <= q_pos[:, None]
    forced = (j == 0) | (j == cur) | (j == cur - 1)
    score = jnp.where((valid & forced)[None, :, None, :], NSA_FORCE,
                      jnp.where(valid[None, :, None, :], imp, -jnp.inf))
    _, idx = lax.top_k(score, min(NSA_TOPN, n_blocks))
    sel = gather_slc(idx)
    ks, vs = sel[..., 0, :], sel[..., 1, :]
    k_pos = idx[..., None] * NSA_SLC_LEN + jnp.arange(NSA_SLC_LEN, dtype=jnp.int32)
    s_s = jnp.einsum('btghd,btgnld->btghnl', q_rot, ks) * NSA_SCALE
    B, Tq, G, H, n, L = s_s.shape
    m_s = (k_pos <= q_pos[None, :, None, None, None]).reshape(B, Tq, G, 1, n * L)
    p_s = masked_softmax(s_s.reshape(B, Tq, G, H, n * L), m_s).reshape(s_s.shape)
    o_s = jnp.einsum('btghnl,btgnld->btghd', p_s.astype(vs.dtype), vs)
    s_w = jnp.einsum('btghd,bsgd->btghs', q_rot, kw) * NSA_SCALE
    dpos = q_pos[:, None] - kw_pos[None, :]
    m_w = (dpos >= 0) & (dpos < NSA_WINDOW) & (kw_pos[None, :] >= 0)
    p_w = masked_softmax(s_w, m_w[None, :, None, None, :])
    o_w = jnp.einsum('btghs,bsgd->btghd', p_w.astype(vw.dtype), vw)
    return gates[..., 0:1] * o_c + gates[..., 1:2] * o_s + gates[..., 2:3] * o_w


def _bg(B):
    return (jnp.arange(B)[:, None, None, None], jnp.arange(NSA_KV_HEADS)[None, None, :, None])


def nsa_prompt(h, w_in, b_gate, w1, b1, w2, w_o):
    B, T = h.shape[:2]
    pos = jnp.arange(T, dtype=jnp.int32)
    q, q_rot, gates, cmp_rows, slc_rows, win_rows = nsa_project(h, pos, w_in, b_gate)
    kc = nsa_compress(cmp_rows[:, :, 0], w1[0], b1[0], w2[0])
    vc = nsa_compress(cmp_rows[:, :, 1], w1[1], b1[1], w2[1])
    o = nsa_prompt_attend_pallas(q, q_rot, gates, kc, vc, slc_rows, win_rows)
    y = o.reshape(B, T, NSA_HEADS * NSA_HD) @ w_o
    wb = min(NSA_WINDOW, T)
    return y, cmp_rows, slc_rows, win_rows[:, T - wb:]


def nsa_sample(h, pool_cmp, pool_slc, win_state, page_table, w_in, b_gate, w1, b1, w2, w_o):
    B, T = h.shape[:2]
    past = page_table.shape[1] * PAGE_SIZE
    pos = past + jnp.arange(T, dtype=jnp.int32)
    q, q_rot, gates, cmp_new, slc_new, win_new = nsa_project(h, pos, w_in, b_gate)
    cmp_all = jnp.concatenate([pool_cmp[page_table].reshape(B, past, 2, NSA_KV_HEADS, NSA_HD), cmp_new], 1)
    kc = nsa_compress(cmp_all[:, :, 0], w1[0], b1[0], w2[0])
    vc = nsa_compress(cmp_all[:, :, 1], w1[1], b1[1], w2[1])
    spp = PAGE_SIZE // NSA_SLC_LEN
    pool_blocks = pool_slc.reshape(pool_slc.shape[0] * spp, NSA_SLC_LEN, 2, NSA_KV_HEADS, NSA_HD)
    ns_past = past // NSA_SLC_LEN
    n_new = -(-T // NSA_SLC_LEN)
    new_blocks = jnp.pad(slc_new, ((0, 0), (0, n_new * NSA_SLC_LEN - T), (0, 0), (0, 0), (0, 0)))
    new_blocks = new_blocks.reshape(B, n_new, NSA_SLC_LEN, 2, NSA_KV_HEADS, NSA_HD)
    bi, gi = _bg(B)

    def gather(idx):
        jp = jnp.minimum(idx, ns_past - 1)
        phys = page_table[bi, jp // spp] * spp + jp % spp
        from_past = pool_blocks[phys, :, :, gi, :]
        jn = jnp.clip(idx - ns_past, 0, n_new - 1)
        from_new = new_blocks[bi, jn, :, :, gi, :]
        return jnp.where((idx < ns_past)[..., None, None, None], from_past, from_new)

    wb = win_state.shape[1]
    win_all = jnp.concatenate([win_state, win_new], 1)
    kw_pos = past - wb + jnp.arange(wb + T, dtype=jnp.int32)
    o = map_query_blocks(
        lambda qs, p: nsa_core(qs[0], qs[1], qs[2], p, kc, vc, gather, ns_past + n_new,
                               win_all[:, :, 0], win_all[:, :, 1], kw_pos),
        (q, q_rot, gates), pos, NSA_Q_BLOCK)
    y = o.reshape(B, T, NSA_HEADS * NSA_HD) @ w_o
    return y, cmp_new, slc_new, win_all[:, T:]


def peer_route(xc, w_q, sub_keys):
    c = xc.shape[0]
    q = (xc @ w_q).reshape(c, PEER_HEADS, 2, PEER_DKEY // 2)
    s = jnp.einsum('chpk,pnk->chpn', q, sub_keys)
    s1, i1 = lax.top_k(s[:, :, 0], PEER_TOPK)
    s2, i2 = lax.top_k(s[:, :, 1], PEER_TOPK)
    cand = (s1[..., :, None] + s2[..., None, :]).reshape(c, PEER_HEADS, PEER_TOPK * PEER_TOPK)
    cidx = (i1[..., :, None] * PEER_NKEYS + i2[..., None, :]).reshape(c, PEER_HEADS, PEER_TOPK * PEER_TOPK)
    top_s, sel = lax.top_k(cand, PEER_TOPK)
    idx = jnp.take_along_axis(cidx, sel, -1)
    g = jax.nn.softmax(top_s.astype(jnp.float32), -1)
    return idx.reshape(c, PEER_PICKS).astype(jnp.int32), g.reshape(c, PEER_PICKS)


def peer(h, w_q, sub_keys, u, v):
    B, T, D = h.shape
    n = B * T
    x = h.reshape(n, D)
    chunk = min(n, PEER_ROUTE_CHUNK)
    idx, g = lax.map(lambda xc: peer_route(xc, w_q, sub_keys), x.reshape(n // chunk, chunk, D))
    out = peer_experts_pallas(x, idx.reshape(n, PEER_PICKS), g.reshape(n, PEER_PICKS), u, v)
    return out.reshape(B, T, D)


def kernel(x_prompt, x_sample, cache_mla_latent, cache_nsa_cmp, cache_nsa_slc, state_nsa_win, page_table,
           norm_mix, norm_ffn, norm_final, mla_w_in, mla_q_norm, mla_kv_norm, mla_w_uq, mla_w_uk, mla_w_uv,
           mla_w_o, nsa_w_in, nsa_b_gate, nsa_phi_w1, nsa_phi_b1, nsa_phi_w2, nsa_w_o, peer_w_q, peer_sub_keys,
           peer_u, peer_v):
    depth = norm_mix.shape[0]
    xp, xs = x_prompt, x_sample
    mla_p, mla_s, cmp_p, cmp_s, slc_p, slc_s, win_p, win_s = [], [], [], [], [], [], [], []
    for i in range(depth):
        j = i // N_MIXERS
        hp = rms_norm(xp, norm_mix[i])
        hs = rms_norm(xs, norm_mix[i])
        if i % N_MIXERS == 0:
            w = (mla_w_in[j], mla_q_norm[j], mla_kv_norm[j], mla_w_uq[j], mla_w_uk[j], mla_w_uv[j], mla_w_o[j])
            op, rp = mla_prompt(hp, *w)
            osm, rs = mla_sample(hs, cache_mla_latent[j], page_table, *w)
            mla_p.append(rp)
            mla_s.append(rs)
        else:
            w = (nsa_w_in[j], nsa_b_gate[j], nsa_phi_w1[j], nsa_phi_b1[j], nsa_phi_w2[j], nsa_w_o[j])
            op, cp, sp, wp = nsa_prompt(hp, *w)
            osm, cs, ss, ws = nsa_sample(hs, cache_nsa_cmp[j], cache_nsa_slc[j], state_nsa_win[j], page_table, *w)
            cmp_p.append(cp)
            cmp_s.append(cs)
            slc_p.append(sp)
            slc_s.append(ss)
            win_p.append(wp)
            win_s.append(ws)
        xp = xp + op
        xs = xs + osm
        pw = (peer_w_q[i], peer_sub_keys[i], peer_u[i], peer_v[i])
        xp = xp + peer(rms_norm(xp, norm_ffn[i]), *pw)
        xs = xs + peer(rms_norm(xs, norm_ffn[i]), *pw)
    y_prompt = rms_norm_pallas(xp, norm_final)
    y_sample = rms_norm_pallas(xs, norm_final)
    return (y_prompt, y_sample, jnp.stack(mla_p), jnp.stack(mla_s), jnp.stack(cmp_p), jnp.stack(cmp_s),
            jnp.stack(slc_p), jnp.stack(slc_s), jnp.stack(win_p), jnp.stack(win_s))
```

```python
import functools

import jax
import jax.numpy as jnp
import numpy as np
from jax import lax
from jax.experimental import pallas as pl
from jax.experimental.pallas import tpu as pltpu

D_MODEL = 1024
PAGE_SIZE = 128
N_MIXERS = 2
NORM_EPS = 1e-6
ROPE_THETA = 10000.0
Q_BLOCK = 128
MLA_HEADS = 8
MLA_Q_LORA = 512
MLA_KV_LORA = 256
MLA_NOPE = 128
MLA_ROPE = 64
MLA_V = 128
MLA_LAT = MLA_KV_LORA + MLA_ROPE
MLA_SCALE = (MLA_NOPE + MLA_ROPE) ** -0.5
NSA_HEADS = 16
NSA_KV_HEADS = 4
NSA_GROUP = NSA_HEADS // NSA_KV_HEADS
NSA_HD = 64
NSA_CMP_LEN = 32
NSA_CMP_STRIDE = 16
NSA_CMP_RATIO = NSA_CMP_LEN // NSA_CMP_STRIDE
NSA_SLC_LEN = 64
NSA_TOPN = 16
NSA_WINDOW = 512
NSA_Q_BLOCK = 64
NSA_SCALE = NSA_HD ** -0.5
NSA_FORCE = 1e9
PEER_HEADS = 8
PEER_NKEYS = 128
PEER_DKEY = 256
PEER_TOPK = 16


def _rms_norm_body(x_ref, g_ref, o_ref):
    x = x_ref[...]
    y = x * lax.rsqrt(jnp.mean(x * x, -1, keepdims=True) + NORM_EPS)
    o_ref[...] = y * g_ref[...]


def rms_norm_pallas(x, g, rows=512):
    shp = x.shape
    x2 = x.reshape(-1, shp[-1])
    n, d = x2.shape
    rows = min(rows, n)
    out = pl.pallas_call(
        _rms_norm_body,
        grid=(n // rows,),
        in_specs=[pl.BlockSpec((rows, d), lambda i: (i, 0)),
                  pl.BlockSpec((1, d), lambda i: (0, 0))],
        out_specs=pl.BlockSpec((rows, d), lambda i: (i, 0)),
        out_shape=jax.ShapeDtypeStruct((n, d), x.dtype),
    )(x2, g.reshape(1, d))
    return out.reshape(shp)


LANES = 128
PEER_TOKENS_PER_STEP = 16
PEER_PICKS = PEER_HEADS * PEER_TOPK
PEER_ISSUE_UNROLL = 32
PEER_VMEM_LIMIT_BYTES = 48 * 1024 * 1024


def _peer_expert_body(idx_cur, idx_nxt, x_ref, g_ref, u_hbm, v_hbm, o_ref, ubuf, vbuf, sem):
    tb, d = x_ref.shape
    picks = g_ref.shape[1]
    rows = tb * picks
    nchunk = d // LANES
    i = pl.program_id(0)
    n = pl.num_programs(0)
    slot = lax.rem(i, 2)

    def row_copies(e, s, r):
        src = pl.ds(pl.multiple_of(e * nchunk, nchunk), nchunk)
        dst = pl.ds(pl.multiple_of(r * nchunk, nchunk), nchunk)
        return (pltpu.make_async_copy(u_hbm.at[src], ubuf.at[s, dst], sem.at[0, s]),
                pltpu.make_async_copy(v_hbm.at[src], vbuf.at[s, dst], sem.at[1, s]))

    def issue(idx_ref, s):
        def pick(r, c):
            cu, cv = row_copies(idx_ref[0, r], s, r)
            cu.start()
            cv.start(priority=1)
            return c
        lax.fori_loop(0, rows, pick, 0, unroll=PEER_ISSUE_UNROLL)

    @pl.when(i == 0)
    def _():
        issue(idx_cur, 0)

    @pl.when(i + 1 < n)
    def _():
        issue(idx_nxt, 1 - slot)

    pltpu.make_async_copy(u_hbm.at[pl.ds(0, rows * nchunk)], ubuf.at[slot], sem.at[0, slot]).wait()
    pltpu.make_async_copy(v_hbm.at[pl.ds(0, rows * nchunk)], vbuf.at[slot], sem.at[1, slot]).wait()

    xb = x_ref[...].astype(jnp.bfloat16)
    g = g_ref[...]
    row_id = lax.broadcasted_iota(jnp.int32, (tb, picks), 0)

    def picked_rows(buf, t):
        base = t * picks * nchunk
        return jnp.concatenate(
            [buf[slot, pl.ds(base + c, picks, stride=nchunk), :].astype(jnp.bfloat16) for c in range(nchunk)], axis=1)

    acc = jnp.zeros((tb, d), jnp.float32)
    for t in range(tb):
        a = lax.dot_general(xb, picked_rows(ubuf, t), (((1,), (1,)), ((), ())), preferred_element_type=jnp.float32)
        w = jnp.where(row_id == t, g * jax.nn.gelu(a), 0.0).astype(jnp.bfloat16)
        acc = acc + jnp.dot(w, picked_rows(vbuf, t), preferred_element_type=jnp.float32)
    o_ref[...] = acc


def peer_experts_pallas(x, idx, g, u, v):
    n, d = x.shape
    picks = idx.shape[1]
    tb = PEER_TOKENS_PER_STEP
    assert n % tb == 0 and u.shape == v.shape and u.shape[1] == d and d % LANES == 0
    steps = n // tb
    rows = tb * picks
    nchunk = d // LANES
    idx3 = idx.reshape(steps, 1, rows)
    u2 = u.reshape(u.shape[0] * nchunk, LANES)
    v2 = v.reshape(v.shape[0] * nchunk, LANES)
    return pl.pallas_call(
        _peer_expert_body,
        grid=(steps,),
        in_specs=[
            pl.BlockSpec((None, 1, rows), lambda i: (i, 0, 0), memory_space=pltpu.SMEM),
            pl.BlockSpec((None, 1, rows), lambda i: (jnp.minimum(i + 1, steps - 1), 0, 0), memory_space=pltpu.SMEM),
            pl.BlockSpec((tb, d), lambda i: (i, 0)),
            pl.BlockSpec((tb, picks), lambda i: (i, 0)),
            pl.BlockSpec(memory_space=pl.ANY),
            pl.BlockSpec(memory_space=pl.ANY),
        ],
        out_specs=pl.BlockSpec((tb, d), lambda i: (i, 0)),
        out_shape=jax.ShapeDtypeStruct((n, d), jnp.float32),
        scratch_shapes=[
            pltpu.VMEM((2, rows * nchunk, LANES), jnp.float32),
            pltpu.VMEM((2, rows * nchunk, LANES), jnp.float32),
            pltpu.SemaphoreType.DMA((2, 2)),
        ],
        compiler_params=pltpu.CompilerParams(
            dimension_semantics=("arbitrary",),
            vmem_limit_bytes=PEER_VMEM_LIMIT_BYTES),
    )(idx3, idx3, x, g, u2, v2)


PEER_ROUTE_TOKENS = 256
PEER_ROUTE_VMEM_LIMIT_BYTES = 40 * 1024 * 1024


def _top_rows(x, k):
    m_rows = x.shape[0]
    rid = lax.broadcasted_iota(jnp.int32, x.shape, 0).astype(jnp.float32)
    vals, ids = [], []
    for _ in range(k):
        m = jnp.max(x, axis=0, keepdims=True)
        am = jnp.min(jnp.where(x == m, rid, float(m_rows)), axis=0, keepdims=True)
        vals.append(m)
        ids.append(am)
        x = jnp.where(rid == am, -jnp.inf, x)
    return jnp.concatenate(vals, axis=0), jnp.concatenate(ids, axis=0)


def _pick_rows(table, pos, k):
    out = jnp.zeros_like(table)
    for a in range(k):
        out = out + jnp.where(pos == float(a), table[a:a + 1], 0.0)
    return out


def _peer_route_body(x_ref, wq_ref, keys_ref, idx_ref, g_ref):
    heads, k, _ = idx_ref.shape
    nkeys, dhalf = keys_ref.shape[1], keys_ref.shape[2]
    nt = (((1,), (1,)), ((), ()))
    xb = x_ref[...].astype(jnp.bfloat16)
    for h in range(heads):
        halves = []
        for p in range(2):
            col = (h * 2 + p) * dhalf
            q = jnp.dot(xb, wq_ref[:, col:col + dhalf], preferred_element_type=jnp.float32)
            s_t = lax.dot_general(keys_ref[p], q.astype(jnp.bfloat16), nt, preferred_element_type=jnp.float32)
            halves.append(_top_rows(s_t, k))
        (s1, i1), (s2, i2) = halves
        cand = jnp.concatenate([s1[a:a + 1] + s2 for a in range(k)], axis=0)
        top_s, sel = _top_rows(cand, k)
        sel_a = jnp.floor(sel * (1.0 / k))
        sel_b = sel - sel_a * k
        ids = _pick_rows(i1, sel_a, k) * nkeys + _pick_rows(i2, sel_b, k)
        e = jnp.exp(top_s - top_s[0:1])
        idx_ref[h] = ids.astype(jnp.int32)
        g_ref[h] = e / jnp.sum(e, axis=0, keepdims=True)


def peer_route_pallas(x, w_q, sub_keys):
    n, d = x.shape
    tbk = min(n, PEER_ROUTE_TOKENS)
    assert n % tbk == 0 and w_q.shape == (d, PEER_HEADS * PEER_DKEY)
    wq = w_q.astype(jnp.bfloat16)
    keys = sub_keys.astype(jnp.bfloat16)
    out_spec = pl.BlockSpec((PEER_HEADS, PEER_TOPK, tbk), lambda i: (0, 0, i))
    idx_t, g_t = pl.pallas_call(
        _peer_route_body,
        grid=(n // tbk,),
        in_specs=[pl.BlockSpec((tbk, d), lambda i: (i, 0)),
                  pl.BlockSpec(wq.shape, lambda i: (0, 0)),
                  pl.BlockSpec(keys.shape, lambda i: (0, 0, 0))],
        out_specs=[out_spec, out_spec],
        out_shape=[jax.ShapeDtypeStruct((PEER_HEADS, PEER_TOPK, n), jnp.int32),
                   jax.ShapeDtypeStruct((PEER_HEADS, PEER_TOPK, n), jnp.float32)],
        compiler_params=pltpu.CompilerParams(
            dimension_semantics=("arbitrary",),
            vmem_limit_bytes=PEER_ROUTE_VMEM_LIMIT_BYTES),
    )(x, wq, keys)
    to_tokens = lambda a: jnp.transpose(a, (2, 0, 1)).reshape(n, PEER_PICKS)
    return to_tokens(idx_t), to_tokens(g_t)


NSA_TQ = 64
NSA_VMEM_LIMIT_BYTES = 48 * 1024 * 1024


def _masked_softmax_rows(s, mask):
    s = jnp.where(mask, s, -jnp.inf)
    m = jnp.max(s, -1, keepdims=True)
    m = jnp.where(m > -jnp.inf, m, 0.0)
    p = jnp.exp(s - m)
    den = jnp.sum(p, -1, keepdims=True)
    return p * (1.0 / jnp.where(den > 0, den, 1.0))


def _nsa_prompt_body(qraw_ref, qrot_ref, gate_ref, kc_ref, vc_ref, ks_ref, vs_ref, kw_ref, vw_ref,
                     cover_ref, expand_ref, o_ref, *, topn):
    hpg, tq, hd = qraw_ref.shape
    rows = hpg * tq
    seq = ks_ref.shape[0]
    ncmp = kc_ref.shape[0]
    nblk = cover_ref.shape[1]
    q0 = pl.multiple_of(pl.program_id(2) * tq, tq)
    nt = (((1,), (1,)), ((), ()))
    bf16 = jnp.bfloat16

    q_raw = qraw_ref[...].reshape(rows, hd).astype(bf16)
    q_rot = qrot_ref[...].reshape(rows, hd).astype(bf16)
    qpos = q0 + (lax.broadcasted_iota(jnp.int32, (rows, 1), 0) & (tq - 1))

    s_c = lax.dot_general(q_raw, kc_ref[...], nt, preferred_element_type=jnp.float32) * NSA_SCALE
    cmp_end = lax.broadcasted_iota(jnp.int32, (rows, ncmp), 1) * NSA_CMP_STRIDE + (NSA_CMP_LEN - 1)
    p_c = _masked_softmax_rows(s_c, cmp_end <= qpos)
    p_cb = p_c.astype(bf16)
    o_c = jnp.dot(p_cb, vc_ref[...], preferred_element_type=jnp.float32)

    cover = cover_ref[...]
    imp = jnp.dot(p_cb[0:tq], cover, preferred_element_type=jnp.float32)
    for h in range(1, hpg):
        imp = imp + jnp.dot(p_cb[h * tq:(h + 1) * tq], cover, preferred_element_type=jnp.float32)
    j_id = lax.broadcasted_iota(jnp.int32, (tq, nblk), 1)
    tpos = q0 + lax.broadcasted_iota(jnp.int32, (tq, nblk), 0)
    cur = lax.shift_right_logical(tpos, NSA_SLC_LEN.bit_length() - 1)
    valid = j_id * NSA_SLC_LEN <= tpos
    forced = (j_id == 0) | (j_id == cur) | (j_id == cur - 1)
    score = jnp.where(valid & forced, NSA_FORCE, jnp.where(valid, imp, -jnp.inf))
    rank = jnp.zeros((tq, nblk), jnp.int32)
    for jp in range(nblk):
        col = score[:, jp:jp + 1]
        before = (col > score) | ((col == score) & (j_id > jp))
        rank = rank + before.astype(jnp.int32)
    sel = jnp.where(rank < topn, 1.0, 0.0).astype(bf16)
    sel_rows = jnp.concatenate([sel] * hpg, axis=0)
    sel_keys = jnp.dot(sel_rows, expand_ref[...], preferred_element_type=jnp.float32)

    s_s = lax.dot_general(q_rot, ks_ref[...], nt, preferred_element_type=jnp.float32) * NSA_SCALE
    kpos = lax.broadcasted_iota(jnp.int32, (rows, seq), 1)
    p_s = _masked_softmax_rows(s_s, (sel_keys > 0.5) & (kpos <= qpos))
    o_s = jnp.dot(p_s.astype(bf16), vs_ref[...], preferred_element_type=jnp.float32)

    band = NSA_WINDOW + tq
    kw = kw_ref[pl.ds(q0, band), :]
    vw = vw_ref[pl.ds(q0, band), :]
    s_w = lax.dot_general(q_rot, kw, nt, preferred_element_type=jnp.float32) * NSA_SCALE
    kwpos = q0 - NSA_WINDOW + lax.broadcasted_iota(jnp.int32, (rows, band), 1)
    dpos = qpos - kwpos
    p_w = _masked_softmax_rows(s_w, (dpos >= 0) & (dpos < NSA_WINDOW) & (kwpos >= 0))
    o_w = jnp.dot(p_w.astype(bf16), vw, preferred_element_type=jnp.float32)

    gts = gate_ref[...].reshape(rows, 3)
    o = gts[:, 0:1] * o_c + gts[:, 1:2] * o_s + gts[:, 2:3] * o_w
    o_ref[...] = o.reshape(hpg, tq, hd)


def nsa_prompt_attend_pallas(q, q_rot, gates, kc, vc, slc_rows, win_rows):
    B, T, G, H, D = q.shape
    nb = kc.shape[1]
    tq = NSA_TQ
    assert T % tq == 0 and tq & (tq - 1) == 0 and T % NSA_SLC_LEN == 0
    nblk = T // NSA_SLC_LEN
    topn = min(NSA_TOPN, nblk)
    ncmp = -(-nb // LANES) * LANES
    bf16 = jnp.bfloat16
    heads_first = lambda a: jnp.transpose(a, (0, 2, 3, 1, 4))
    kv_first = lambda a: jnp.transpose(a, (0, 2, 1, 3)).astype(bf16)
    pad_cmp = lambda a: jnp.pad(kv_first(a), ((0, 0), (0, 0), (0, ncmp - nb), (0, 0)))
    pad_win = lambda a: jnp.pad(kv_first(a), ((0, 0), (0, 0), (NSA_WINDOW, 0), (0, 0)))
    ci = np.arange(ncmp)[:, None] * NSA_CMP_STRIDE
    sj = np.arange(nblk)[None, :] * NSA_SLC_LEN
    cover = jnp.asarray((ci < sj + NSA_SLC_LEN) & (ci + NSA_CMP_LEN > sj) & (np.arange(ncmp)[:, None] < nb), bf16)
    expand = jnp.asarray(np.arange(nblk)[:, None] == (np.arange(T)[None, :] // NSA_SLC_LEN), bf16)

    qspec = lambda last: pl.BlockSpec((None, None, H, tq, last), lambda b, g, t: (b, g, 0, t, 0))
    kvspec = lambda s: pl.BlockSpec((None, None, s, D), lambda b, g, t: (b, g, 0, 0))
    const = lambda a: pl.BlockSpec(a.shape, lambda b, g, t: (0, 0))
    out = pl.pallas_call(
        functools.partial(_nsa_prompt_body, topn=topn),
        grid=(B, G, T // tq),
        in_specs=[qspec(D), qspec(D), qspec(3), kvspec(ncmp), kvspec(ncmp), kvspec(T), kvspec(T),
                  kvspec(T + NSA_WINDOW), kvspec(T + NSA_WINDOW), const(cover), const(expand)],
        out_specs=qspec(D),
        out_shape=jax.ShapeDtypeStruct((B, G, H, T, D), jnp.float32),
        compiler_params=pltpu.CompilerParams(
            dimension_semantics=("arbitrary", "arbitrary", "arbitrary"),
            vmem_limit_bytes=NSA_VMEM_LIMIT_BYTES),
    )(heads_first(q), heads_first(q_rot), heads_first(gates), pad_cmp(kc), pad_cmp(vc),
      kv_first(slc_rows[:, :, 0]), kv_first(slc_rows[:, :, 1]),
      pad_win(win_rows[:, :, 0]), pad_win(win_rows[:, :, 1]), cover, expand)
    return jnp.transpose(out, (0, 3, 1, 2, 4))


def rms_norm(x, g):
    xf = x.astype(jnp.float32)
    y = xf * lax.rsqrt(jnp.mean(xf * xf, -1, keepdims=True) + NORM_EPS)
    return (y * g.astype(jnp.float32)).astype(x.dtype)


def rope(x, pos):
    half = x.shape[-1] // 2
    inv = ROPE_THETA ** (-jnp.arange(half, dtype=jnp.float32) / half)
    ang = pos.astype(jnp.float32)[:, None] * inv[None, :]
    shp = (pos.shape[0],) + (1,) * (x.ndim - 3) + (half,)
    cos, sin = jnp.cos(ang).reshape(shp), jnp.sin(ang).reshape(shp)
    xf = x.astype(jnp.float32)
    x1, x2 = xf[..., :half], xf[..., half:]
    return jnp.concatenate([x1 * cos - x2 * sin, x2 * cos + x1 * sin], -1).astype(x.dtype)


def masked_softmax(s, mask):
    s = jnp.where(mask, s.astype(jnp.float32), -jnp.inf)
    m = jnp.max(s, -1, keepdims=True)
    m = jnp.where(jnp.isfinite(m), m, 0.0)
    p = jnp.exp(s - m)
    den = jnp.sum(p, -1, keepdims=True)
    return p / jnp.where(den > 0, den, 1.0)


def map_query_blocks(fn, qs, q_pos, blk):
    T = q_pos.shape[0]
    if T <= blk:
        return fn(qs, q_pos)
    nb = -(-T // blk)
    pad = nb * blk - T

    def split(a):
        a = jnp.pad(a, [(0, 0), (0, pad)] + [(0, 0)] * (a.ndim - 2))
        return jnp.moveaxis(a.reshape(a.shape[0], nb, blk, *a.shape[2:]), 1, 0)

    qb = tuple(split(a) for a in qs)
    pb = jnp.pad(q_pos, (0, pad), mode='edge').reshape(nb, blk)
    out = lax.map(lambda a: fn(a[0], a[1]), (qb, pb))
    out = jnp.moveaxis(out, 0, 1)
    return out.reshape(out.shape[0], nb * blk, *out.shape[3:])[:, :T]


def mla_project(h, pos, w_in, q_norm, kv_norm, w_uq, w_uk):
    z = h @ w_in
    cq = rms_norm(z[..., :MLA_Q_LORA], q_norm)
    ckv = rms_norm(z[..., MLA_Q_LORA:MLA_Q_LORA + MLA_KV_LORA], kv_norm)
    kr = rope(z[..., MLA_Q_LORA + MLA_KV_LORA:][:, :, None, :], pos)[:, :, 0]
    lat = jnp.concatenate([ckv, kr], -1)
    q = jnp.einsum('btr,rhd->bthd', cq, w_uq)
    q_lat = jnp.einsum('bthn,rhn->bthr', q[..., :MLA_NOPE], w_uk)
    q_abs = jnp.concatenate([q_lat, rope(q[..., MLA_NOPE:], pos)], -1)
    return lat, q_abs


def latent_attend(q_abs, q_pos, lat, k_pos):
    s = jnp.einsum('bthc,bsc->bhts', q_abs, lat) * MLA_SCALE
    p = masked_softmax(s, (k_pos[None, :] <= q_pos[:, None])[None, None])
    return jnp.einsum('bhts,bsr->bthr', p.astype(lat.dtype), lat[..., :MLA_KV_LORA])


def mla_out(o_lat, w_uv, w_o):
    B, T = o_lat.shape[:2]
    o = jnp.einsum('bthr,rhv->bthv', o_lat, w_uv)
    return o.reshape(B, T, MLA_HEADS * MLA_V) @ w_o


def mla_prompt(h, w_in, q_norm, kv_norm, w_uq, w_uk, w_uv, w_o):
    T = h.shape[1]
    pos = jnp.arange(T, dtype=jnp.int32)
    lat, q_abs = mla_project(h, pos, w_in, q_norm, kv_norm, w_uq, w_uk)
    o = map_query_blocks(lambda qs, p: latent_attend(qs[0], p, lat, pos), (q_abs,), pos, Q_BLOCK)
    return mla_out(o, w_uv, w_o), lat


def mla_sample(h, pool, page_table, w_in, q_norm, kv_norm, w_uq, w_uk, w_uv, w_o):
    B, T = h.shape[:2]
    past = page_table.shape[1] * PAGE_SIZE
    pos = past + jnp.arange(T, dtype=jnp.int32)
    lat_new, q_abs = mla_project(h, pos, w_in, q_norm, kv_norm, w_uq, w_uk)
    lat = jnp.concatenate([pool[page_table].reshape(B, past, MLA_LAT), lat_new], 1)
    k_pos = jnp.arange(past + T, dtype=jnp.int32)
    o = map_query_blocks(lambda qs, p: latent_attend(qs[0], p, lat, k_pos), (q_abs,), pos, Q_BLOCK)
    return mla_out(o, w_uv, w_o), lat_new


def nsa_project(h, pos, w_in, b_gate):
    B, T = h.shape[:2]
    nq = NSA_HEADS * NSA_HD
    nkv = 6 * NSA_KV_HEADS * NSA_HD
    z = h @ w_in
    q = z[..., :nq].reshape(B, T, NSA_KV_HEADS, NSA_GROUP, NSA_HD)
    kv = z[..., nq:nq + nkv].reshape(B, T, 3, 2, NSA_KV_HEADS, NSA_HD)
    gates = jax.nn.sigmoid(z[..., nq + nkv:] + b_gate).reshape(B, T, NSA_KV_HEADS, NSA_GROUP, 3)
    q_rot = rope(q, pos)

    def rot_k(r):
        return jnp.stack([rope(r[:, :, 0], pos), r[:, :, 1]], axis=2)

    return q, q_rot, gates, kv[:, :, 0], rot_k(kv[:, :, 1]), rot_k(kv[:, :, 2])


def nsa_compress(rows, w1, b1, w2):
    B, T = rows.shape[:2]
    nch = -(-T // NSA_CMP_STRIDE)
    rows = jnp.pad(rows, ((0, 0), (0, nch * NSA_CMP_STRIDE - T), (0, 0), (0, 0)))
    ch = rows.reshape(B, nch, NSA_CMP_STRIDE, NSA_KV_HEADS, NSA_HD)
    nb = nch - NSA_CMP_RATIO + 1
    pre = b1
    for r in range(NSA_CMP_RATIO):
        w1r = w1[r * NSA_CMP_STRIDE:(r + 1) * NSA_CMP_STRIDE]
        pre = pre + jnp.einsum('bclgd,ldh->bcgh', ch[:, r:r + nb], w1r)
    return jnp.einsum('bngh,hd->bngd', jax.nn.gelu(pre), w2)


def nsa_core(q_raw, q_rot, gates, q_pos, kc, vc, gather_slc, n_blocks, kw, vw, kw_pos):
    nb = kc.shape[1]
    cmp_end = jnp.arange(nb, dtype=jnp.int32) * NSA_CMP_STRIDE + NSA_CMP_LEN - 1
    s_c = jnp.einsum('btghd,bngd->btghn', q_raw, kc) * NSA_SCALE
    p_c = masked_softmax(s_c, (cmp_end[None, :] <= q_pos[:, None])[None, :, None, None, :])
    o_c = jnp.einsum('btghn,bngd->btghd', p_c.astype(vc.dtype), vc)
    ci = jnp.arange(nb, dtype=jnp.int32)[:, None] * NSA_CMP_STRIDE
    sj = jnp.arange(n_blocks, dtype=jnp.int32)[None, :] * NSA_SLC_LEN
    cover = ((ci < sj + NSA_SLC_LEN) & (ci + NSA_CMP_LEN > sj)).astype(jnp.float32)
    imp = jnp.einsum('btghn,nj->btgj', p_c, cover)
    j = jnp.arange(n_blocks, dtype=jnp.int32)[None, :]
    cur = (q_pos // NSA_SLC_LEN)[:, None]
    valid = j * NSA_SLC_LEN <= q_pos[:, None]
    forced = (j == 0) | (j == cur) | (j == cur - 1)
    score = jnp.where((valid & forced)[None, :, None, :], NSA_FORCE,
                      jnp.where(valid[None, :, None, :], imp, -jnp.inf))
    _, idx = lax.top_k(score, min(NSA_TOPN, n_blocks))
    sel = gather_slc(idx)
    ks, vs = sel[..., 0, :], sel[..., 1, :]
    k_pos = idx[..., None] * NSA_SLC_LEN + jnp.arange(NSA_SLC_LEN, dtype=jnp.int32)
    s_s = jnp.einsum('btghd,btgnld->btghnl', q_rot, ks) * NSA_SCALE
    B, Tq, G, H, n, L = s_s.shape
    m_s = (k_pos <= q_pos[None, :, None, None, None]).reshape(B, Tq, G, 1, n * L)
    p_s = masked_softmax(s_s.reshape(B, Tq, G, H, n * L), m_s).reshape(s_s.shape)
    o_s = jnp.einsum('btghnl,btgnld->btghd', p_s.astype(vs.dtype), vs)
    s_w = jnp.einsum('btghd,bsgd->btghs', q_rot, kw) * NSA_SCALE
    dpos = q_pos[:, None] - kw_pos[None, :]
    m_w = (dpos >= 0) & (dpos < NSA_WINDOW) & (kw_pos[None, :] >= 0)
    p_w = masked_softmax(s_w, m_w[None, :, None, None, :])
    o_w = jnp.einsum('btghs,bsgd->btghd', p_w.astype(vw.dtype), vw)
    return gates[..., 0:1] * o_c + gates[..., 1:2] * o_s + gates[..., 2:3] * o_w


def _bg(B):
    return (jnp.arange(B)[:, None, None, None], jnp.arange(NSA_KV_HEADS)[None, None, :, None])


def nsa_prompt(h, w_in, b_gate, w1, b1, w2, w_o):
    B, T = h.shape[:2]
    pos = jnp.arange(T, dtype=jnp.int32)
    q, q_rot, gates, cmp_rows, slc_rows, win_rows = nsa_project(h, pos, w_in, b_gate)
    kc = nsa_compress(cmp_rows[:, :, 0], w1[0], b1[0], w2[0])
    vc = nsa_compress(cmp_rows[:, :, 1], w1[1], b1[1], w2[1])
    o = nsa_prompt_attend_pallas(q, q_rot, gates, kc, vc, slc_rows, win_rows)
    y = o.reshape(B, T, NSA_HEADS * NSA_HD) @ w_o
    wb = min(NSA_WINDOW, T)
    return y, cmp_rows, slc_rows, win_rows[:, T - wb:]


def nsa_sample(h, pool_cmp, pool_slc, win_state, page_table, w_in, b_gate, w1, b1, w2, w_o):
    B, T = h.shape[:2]
    past = page_table.shape[1] * PAGE_SIZE
    pos = past + jnp.arange(T, dtype=jnp.int32)
    q, q_rot, gates, cmp_new, slc_new, win_new = nsa_project(h, pos, w_in, b_gate)
    cmp_all = jnp.concatenate([pool_cmp[page_table].reshape(B, past, 2, NSA_KV_HEADS, NSA_HD), cmp_new], 1)
    kc = nsa_compress(cmp_all[:, :, 0], w1[0], b1[0], w2[0])
    vc = nsa_compress(cmp_all[:, :, 1], w1[1], b1[1], w2[1])
    spp = PAGE_SIZE // NSA_SLC_LEN
    pool_blocks = pool_slc.reshape(pool_slc.shape[0] * spp, NSA_SLC_LEN, 2, NSA_KV_HEADS, NSA_HD)
    ns_past = past // NSA_SLC_LEN
    n_new = -(-T // NSA_SLC_LEN)
    new_blocks = jnp.pad(slc_new, ((0, 0), (0, n_new * NSA_SLC_LEN - T), (0, 0), (0, 0), (0, 0)))
    new_blocks = new_blocks.reshape(B, n_new, NSA_SLC_LEN, 2, NSA_KV_HEADS, NSA_HD)
    bi, gi = _bg(B)

    def gather(idx):
        jp = jnp.minimum(idx, ns_past - 1)
        phys = page_table[bi, jp // spp] * spp + jp % spp
        from_past = pool_blocks[phys, :, :, gi, :]
        jn = jnp.clip(idx - ns_past, 0, n_new - 1)
        from_new = new_blocks[bi, jn, :, :, gi, :]
        return jnp.where((idx < ns_past)[..., None, None, None], from_past, from_new)

    wb = win_state.shape[1]
    win_all = jnp.concatenate([win_state, win_new], 1)
    kw_pos = past - wb + jnp.arange(wb + T, dtype=jnp.int32)
    o = map_query_blocks(
        lambda qs, p: nsa_core(qs[0], qs[1], qs[2], p, kc, vc, gather, ns_past + n_new,
                               win_all[:, :, 0], win_all[:, :, 1], kw_pos),
        (q, q_rot, gates), pos, NSA_Q_BLOCK)
    y = o.reshape(B, T, NSA_HEADS * NSA_HD) @ w_o
    return y, cmp_new, slc_new, win_all[:, T:]


def peer(h, w_q, sub_keys, u, v):
    B, T, D = h.shape
    n = B * T
    x = h.reshape(n, D)
    idx, g = peer_route_pallas(x, w_q, sub_keys)
    out = peer_experts_pallas(x, idx, g, u, v)
    return out.reshape(B, T, D)


def kernel(x_prompt, x_sample, cache_mla_latent, cache_nsa_cmp, cache_nsa_slc, state_nsa_win, page_table,
           norm_mix, norm_ffn, norm_final, mla_w_in, mla_q_norm, mla_kv_norm, mla_w_uq, mla_w_uk, mla_w_uv,
           mla_w_o, nsa_w_in, nsa_b_gate, nsa_phi_w1, nsa_phi_b1, nsa_phi_w2, nsa_w_o, peer_w_q, peer_sub_keys,
           peer_u, peer_v):
    depth = norm_mix.shape[0]
    xp, xs = x_prompt, x_sample
    mla_p, mla_s, cmp_p, cmp_s, slc_p, slc_s, win_p, win_s = [], [], [], [], [], [], [], []
    for i in range(depth):
        j = i // N_MIXERS
        hp = rms_norm(xp, norm_mix[i])
        hs = rms_norm(xs, norm_mix[i])
        if i % N_MIXERS == 0:
            w = (mla_w_in[j], mla_q_norm[j], mla_kv_norm[j], mla_w_uq[j], mla_w_uk[j], mla_w_uv[j], mla_w_o[j])
            op, rp = mla_prompt(hp, *w)
            osm, rs = mla_sample(hs, cache_mla_latent[j], page_table, *w)
            mla_p.append(rp)
            mla_s.append(rs)
        else:
            w = (nsa_w_in[j], nsa_b_gate[j], nsa_phi_w1[j], nsa_phi_b1[j], nsa_phi_w2[j], nsa_w_o[j])
            op, cp, sp, wp = nsa_prompt(hp, *w)
            osm, cs, ss, ws = nsa_sample(hs, cache_nsa_cmp[j], cache_nsa_slc[j], state_nsa_win[j], page_table, *w)
            cmp_p.append(cp)
            cmp_s.append(cs)
            slc_p.append(sp)
            slc_s.append(ss)
            win_p.append(wp)
            win_s.append(ws)
        xp = xp + op
        xs = xs + osm
        pw = (peer_w_q[i], peer_sub_keys[i], peer_u[i], peer_v[i])
        xp = xp + peer(rms_norm(xp, norm_ffn[i]), *pw)
        xs = xs + peer(rms_norm(xs, norm_ffn[i]), *pw)
    y_prompt = rms_norm_pallas(xp, norm_final)
    y_sample = rms_norm_pallas(xs, norm_final)
    return (y_prompt, y_sample, jnp.stack(mla_p), jnp.stack(mla_s), jnp.stack(cmp_p), jnp.stack(cmp_s),
            jnp.stack(slc_p), jnp.stack(slc_s), jnp.stack(win_p), jnp.stack(win_s))
```

```python
import functools

import jax
import jax.numpy as jnp
import numpy as np
from jax import lax
from jax.experimental import pallas as pl
from jax.experimental.pallas import tpu as pltpu

D_MODEL = 1024
PAGE_SIZE = 128
N_MIXERS = 2
NORM_EPS = 1e-6
ROPE_THETA = 10000.0
Q_BLOCK = 128
MLA_HEADS = 8
MLA_Q_LORA = 512
MLA_KV_LORA = 256
MLA_NOPE = 128
MLA_ROPE = 64
MLA_V = 128
MLA_LAT = MLA_KV_LORA + MLA_ROPE
MLA_SCALE = (MLA_NOPE + MLA_ROPE) ** -0.5
NSA_HEADS = 16
NSA_KV_HEADS = 4
NSA_GROUP = NSA_HEADS // NSA_KV_HEADS
NSA_HD = 64
NSA_CMP_LEN = 32
NSA_CMP_STRIDE = 16
NSA_CMP_RATIO = NSA_CMP_LEN // NSA_CMP_STRIDE
NSA_SLC_LEN = 64
NSA_TOPN = 16
NSA_WINDOW = 512
NSA_Q_BLOCK = 64
NSA_SCALE = NSA_HD ** -0.5
NSA_FORCE = 1e9
PEER_HEADS = 8
PEER_NKEYS = 128
PEER_DKEY = 256
PEER_TOPK = 16


def _rms_norm_body(x_ref, g_ref, o_ref):
    x = x_ref[...]
    y = x * lax.rsqrt(jnp.mean(x * x, -1, keepdims=True) + NORM_EPS)
    o_ref[...] = y * g_ref[...]


def rms_norm_pallas(x, g, rows=512):
    shp = x.shape
    x2 = x.reshape(-1, shp[-1])
    n, d = x2.shape
    rows = min(rows, n)
    out = pl.pallas_call(
        _rms_norm_body,
        grid=(n // rows,),
        in_specs=[pl.BlockSpec((rows, d), lambda i: (i, 0)),
                  pl.BlockSpec((1, d), lambda i: (0, 0))],
        out_specs=pl.BlockSpec((rows, d), lambda i: (i, 0)),
        out_shape=jax.ShapeDtypeStruct((n, d), x.dtype),
    )(x2, g.reshape(1, d))
    return out.reshape(shp)


LANES = 128
PEER_TOKENS_PER_STEP = 16
PEER_PICKS = PEER_HEADS * PEER_TOPK
PEER_ISSUE_UNROLL = 32
PEER_VMEM_LIMIT_BYTES = 48 * 1024 * 1024


def _peer_expert_body(idx_cur, idx_nxt, x_ref, g_ref, uv_hbm, o_ref, buf, sem):
    tb, d = x_ref.shape
    picks = g_ref.shape[1]
    rows = tb * picks
    nchunk = d // LANES
    pair = 2 * nchunk
    i = pl.program_id(0)
    n = pl.num_programs(0)
    slot = lax.rem(i, 2)

    def issue(idx_ref, s):
        def pick(r, c):
            src = pl.ds(pl.multiple_of(idx_ref[0, r] * pair, pair), pair)
            dst = pl.ds(pl.multiple_of(r * pair, pair), pair)
            pltpu.make_async_copy(uv_hbm.at[src], buf.at[s, dst], sem.at[s]).start()
            return c
        lax.fori_loop(0, rows, pick, 0, unroll=PEER_ISSUE_UNROLL)

    @pl.when(i == 0)
    def _():
        issue(idx_cur, 0)

    @pl.when(i + 1 < n)
    def _():
        issue(idx_nxt, 1 - slot)

    pltpu.make_async_copy(uv_hbm.at[pl.ds(0, rows * pair)], buf.at[slot], sem.at[slot]).wait()

    xb = x_ref[...].astype(jnp.bfloat16)
    g = g_ref[...]
    row_id = lax.broadcasted_iota(jnp.int32, (tb, picks), 0)

    def picked_rows(off, t):
        base = t * picks * pair + off
        return jnp.concatenate(
            [buf[slot, pl.ds(base + c, picks, stride=pair), :].astype(jnp.bfloat16) for c in range(nchunk)], axis=1)

    acc = jnp.zeros((tb, d), jnp.float32)
    for t in range(tb):
        a = lax.dot_general(xb, picked_rows(0, t), (((1,), (1,)), ((), ())), preferred_element_type=jnp.float32)
        w = jnp.where(row_id == t, g * jax.nn.gelu(a), 0.0).astype(jnp.bfloat16)
        acc = acc + jnp.dot(w, picked_rows(nchunk, t), preferred_element_type=jnp.float32)
    o_ref[...] = acc


def peer_experts_pallas(x, idx, g, u, v):
    n, d = x.shape
    picks = idx.shape[1]
    tb = PEER_TOKENS_PER_STEP
    assert n % tb == 0 and u.shape == v.shape and u.shape[1] == d and d % LANES == 0
    steps = n // tb
    rows = tb * picks
    nchunk = d // LANES
    idx3 = idx.reshape(steps, 1, rows)
    e = u.shape[0]
    uv = jnp.concatenate([u.reshape(e, nchunk, LANES), v.reshape(e, nchunk, LANES)], axis=1).reshape(e * 2 * nchunk, LANES)
    return pl.pallas_call(
        _peer_expert_body,
        grid=(steps,),
        in_specs=[
            pl.BlockSpec((None, 1, rows), lambda i: (i, 0, 0), memory_space=pltpu.SMEM),
            pl.BlockSpec((None, 1, rows), lambda i: (jnp.minimum(i + 1, steps - 1), 0, 0), memory_space=pltpu.SMEM),
            pl.BlockSpec((tb, d), lambda i: (i, 0)),
            pl.BlockSpec((tb, picks), lambda i: (i, 0)),
            pl.BlockSpec(memory_space=pl.ANY),
        ],
        out_specs=pl.BlockSpec((tb, d), lambda i: (i, 0)),
        out_shape=jax.ShapeDtypeStruct((n, d), jnp.float32),
        scratch_shapes=[
            pltpu.VMEM((2, rows * 2 * nchunk, LANES), jnp.float32),
            pltpu.SemaphoreType.DMA((2,)),
        ],
        compiler_params=pltpu.CompilerParams(
            dimension_semantics=("arbitrary",),
            vmem_limit_bytes=PEER_VMEM_LIMIT_BYTES),
    )(idx3, idx3, x, g, uv)


PEER_ROUTE_TOKENS = 256
PEER_ROUTE_VMEM_LIMIT_BYTES = 40 * 1024 * 1024


def _top_rows(x, k):
    m_rows = x.shape[0]
    rid = lax.broadcasted_iota(jnp.int32, x.shape, 0).astype(jnp.float32)
    vals, ids = [], []
    for _ in range(k):
        m = jnp.max(x, axis=0, keepdims=True)
        am = jnp.min(jnp.where(x == m, rid, float(m_rows)), axis=0, keepdims=True)
        vals.append(m)
        ids.append(am)
        x = jnp.where(rid == am, -jnp.inf, x)
    return jnp.concatenate(vals, axis=0), jnp.concatenate(ids, axis=0)


def _pick_rows(table, pos, k):
    out = jnp.zeros_like(table)
    for a in range(k):
        out = out + jnp.where(pos == float(a), table[a:a + 1], 0.0)
    return out


def _peer_route_body(x_ref, wq_ref, keys_ref, idx_ref, g_ref):
    heads, k, _ = idx_ref.shape
    nkeys, dhalf = keys_ref.shape[1], keys_ref.shape[2]
    nt = (((1,), (1,)), ((), ()))
    xb = x_ref[...].astype(jnp.bfloat16)
    for h in range(heads):
        halves = []
        for p in range(2):
            col = (h * 2 + p) * dhalf
            q = jnp.dot(xb, wq_ref[:, col:col + dhalf], preferred_element_type=jnp.float32)
            s_t = lax.dot_general(keys_ref[p], q.astype(jnp.bfloat16), nt, preferred_element_type=jnp.float32)
            halves.append(_top_rows(s_t, k))
        (s1, i1), (s2, i2) = halves
        cand = jnp.concatenate([s1[a:a + 1] + s2 for a in range(k)], axis=0)
        top_s, sel = _top_rows(cand, k)
        sel_a = jnp.floor(sel * (1.0 / k))
        sel_b = sel - sel_a * k
        ids = _pick_rows(i1, sel_a, k) * nkeys + _pick_rows(i2, sel_b, k)
        e = jnp.exp(top_s - top_s[0:1])
        idx_ref[h] = ids.astype(jnp.int32)
        g_ref[h] = e / jnp.sum(e, axis=0, keepdims=True)


def peer_route_pallas(x, w_q, sub_keys):
    n, d = x.shape
    tbk = min(n, PEER_ROUTE_TOKENS)
    assert n % tbk == 0 and w_q.shape == (d, PEER_HEADS * PEER_DKEY)
    wq = w_q.astype(jnp.bfloat16)
    keys = sub_keys.astype(jnp.bfloat16)
    out_spec = pl.BlockSpec((PEER_HEADS, PEER_TOPK, tbk), lambda i: (0, 0, i))
    idx_t, g_t = pl.pallas_call(
        _peer_route_body,
        grid=(n // tbk,),
        in_specs=[pl.BlockSpec((tbk, d), lambda i: (i, 0)),
                  pl.BlockSpec(wq.shape, lambda i: (0, 0)),
                  pl.BlockSpec(keys.shape, lambda i: (0, 0, 0))],
        out_specs=[out_spec, out_spec],
        out_shape=[jax.ShapeDtypeStruct((PEER_HEADS, PEER_TOPK, n), jnp.int32),
                   jax.ShapeDtypeStruct((PEER_HEADS, PEER_TOPK, n), jnp.float32)],
        compiler_params=pltpu.CompilerParams(
            dimension_semantics=("arbitrary",),
            vmem_limit_bytes=PEER_ROUTE_VMEM_LIMIT_BYTES),
    )(x, wq, keys)
    to_tokens = lambda a: jnp.transpose(a, (2, 0, 1)).reshape(n, PEER_PICKS)
    return to_tokens(idx_t), to_tokens(g_t)


NSA_TQ = 64
NSA_VMEM_LIMIT_BYTES = 48 * 1024 * 1024


def _masked_softmax_rows(s, mask):
    s = jnp.where(mask, s, -jnp.inf)
    m = jnp.max(s, -1, keepdims=True)
    m = jnp.where(m > -jnp.inf, m, 0.0)
    p = jnp.exp(s - m)
    den = jnp.sum(p, -1, keepdims=True)
    return p * (1.0 / jnp.where(den > 0, den, 1.0))


def _nsa_prompt_body(qraw_ref, qrot_ref, gate_ref, kc_ref, vc_ref, ks_ref, vs_ref, kw_ref, vw_ref,
                     cover_ref, expand_ref, o_ref, *, topn):
    hpg, tq, hd = qraw_ref.shape
    rows = hpg * tq
    seq = ks_ref.shape[0]
    ncmp = kc_ref.shape[0]
    nblk = cover_ref.shape[1]
    q0 = pl.multiple_of(pl.program_id(2) * tq, tq)
    nt = (((1,), (1,)), ((), ()))
    bf16 = jnp.bfloat16

    q_raw = qraw_ref[...].reshape(rows, hd).astype(bf16)
    q_rot = qrot_ref[...].reshape(rows, hd).astype(bf16)
    qpos = q0 + (lax.broadcasted_iota(jnp.int32, (rows, 1), 0) & (tq - 1))

    s_c = lax.dot_general(q_raw, kc_ref[...], nt, preferred_element_type=jnp.float32) * NSA_SCALE
    cmp_end = lax.broadcasted_iota(jnp.int32, (rows, ncmp), 1) * NSA_CMP_STRIDE + (NSA_CMP_LEN - 1)
    p_c = _masked_softmax_rows(s_c, cmp_end <= qpos)
    p_cb = p_c.astype(bf16)
    o_c = jnp.dot(p_cb, vc_ref[...], preferred_element_type=jnp.float32)

    cover = cover_ref[...]
    imp = jnp.dot(p_cb[0:tq], cover, preferred_element_type=jnp.float32)
    for h in range(1, hpg):
        imp = imp + jnp.dot(p_cb[h * tq:(h + 1) * tq], cover, preferred_element_type=jnp.float32)
    j_id = lax.broadcasted_iota(jnp.int32, (tq, nblk), 1)
    tpos = q0 + lax.broadcasted_iota(jnp.int32, (tq, nblk), 0)
    cur = lax.shift_right_logical(tpos, NSA_SLC_LEN.bit_length() - 1)
    valid = j_id * NSA_SLC_LEN <= tpos
    forced = (j_id == 0) | (j_id == cur) | (j_id == cur - 1)
    score = jnp.where(valid & forced, NSA_FORCE, jnp.where(valid, imp, -jnp.inf))
    rank = jnp.zeros((tq, nblk), jnp.int32)
    for jp in range(nblk):
        col = score[:, jp:jp + 1]
        before = (col > score) | ((col == score) & (j_id > jp))
        rank = rank + before.astype(jnp.int32)
    sel = jnp.where(rank < topn, 1.0, 0.0).astype(bf16)
    sel_rows = jnp.concatenate([sel] * hpg, axis=0)
    sel_keys = jnp.dot(sel_rows, expand_ref[...], preferred_element_type=jnp.float32)

    s_s = lax.dot_general(q_rot, ks_ref[...], nt, preferred_element_type=jnp.float32) * NSA_SCALE
    kpos = lax.broadcasted_iota(jnp.int32, (rows, seq), 1)
    p_s = _masked_softmax_rows(s_s, (sel_keys > 0.5) & (kpos <= qpos))
    o_s = jnp.dot(p_s.astype(bf16), vs_ref[...], preferred_element_type=jnp.float32)

    band = NSA_WINDOW + tq
    kw = kw_ref[pl.ds(q0, band), :]
    vw = vw_ref[pl.ds(q0, band), :]
    s_w = lax.dot_general(q_rot, kw, nt, preferred_element_type=jnp.float32) * NSA_SCALE
    kwpos = q0 - NSA_WINDOW + lax.broadcasted_iota(jnp.int32, (rows, band), 1)
    dpos = qpos - kwpos
    p_w = _masked_softmax_rows(s_w, (dpos >= 0) & (dpos < NSA_WINDOW) & (kwpos >= 0))
    o_w = jnp.dot(p_w.astype(bf16), vw, preferred_element_type=jnp.float32)

    gts = gate_ref[...].reshape(rows, 3)
    o = gts[:, 0:1] * o_c + gts[:, 1:2] * o_s + gts[:, 2:3] * o_w
    o_ref[...] = o.reshape(hpg, tq, hd)


def nsa_prompt_attend_pallas(q, q_rot, gates, kc, vc, slc_rows, win_rows):
    B, T, G, H, D = q.shape
    nb = kc.shape[1]
    tq = NSA_TQ
    assert T % tq == 0 and tq & (tq - 1) == 0 and T % NSA_SLC_LEN == 0
    nblk = T // NSA_SLC_LEN
    topn = min(NSA_TOPN, nblk)
    ncmp = -(-nb // LANES) * LANES
    bf16 = jnp.bfloat16
    heads_first = lambda a: jnp.transpose(a, (0, 2, 3, 1, 4))
    kv_first = lambda a: jnp.transpose(a, (0, 2, 1, 3)).astype(bf16)
    pad_cmp = lambda a: jnp.pad(kv_first(a), ((0, 0), (0, 0), (0, ncmp - nb), (0, 0)))
    pad_win = lambda a: jnp.pad(kv_first(a), ((0, 0), (0, 0), (NSA_WINDOW, 0), (0, 0)))
    ci = np.arange(ncmp)[:, None] * NSA_CMP_STRIDE
    sj = np.arange(nblk)[None, :] * NSA_SLC_LEN
    cover = jnp.asarray((ci < sj + NSA_SLC_LEN) & (ci + NSA_CMP_LEN > sj) & (np.arange(ncmp)[:, None] < nb), bf16)
    expand = jnp.asarray(np.arange(nblk)[:, None] == (np.arange(T)[None, :] // NSA_SLC_LEN), bf16)

    qspec = lambda last: pl.BlockSpec((None, None, H, tq, last), lambda b, g, t: (b, g, 0, t, 0))
    kvspec = lambda s: pl.BlockSpec((None, None, s, D), lambda b, g, t: (b, g, 0, 0))
    const = lambda a: pl.BlockSpec(a.shape, lambda b, g, t: (0, 0))
    out = pl.pallas_call(
        functools.partial(_nsa_prompt_body, topn=topn),
        grid=(B, G, T // tq),
        in_specs=[qspec(D), qspec(D), qspec(3), kvspec(ncmp), kvspec(ncmp), kvspec(T), kvspec(T),
                  kvspec(T + NSA_WINDOW), kvspec(T + NSA_WINDOW), const(cover), const(expand)],
        out_specs=qspec(D),
        out_shape=jax.ShapeDtypeStruct((B, G, H, T, D), jnp.float32),
        compiler_params=pltpu.CompilerParams(
            dimension_semantics=("arbitrary", "arbitrary", "arbitrary"),
            vmem_limit_bytes=NSA_VMEM_LIMIT_BYTES),
    )(heads_first(q), heads_first(q_rot), heads_first(gates), pad_cmp(kc), pad_cmp(vc),
      kv_first(slc_rows[:, :, 0]), kv_first(slc_rows[:, :, 1]),
      pad_win(win_rows[:, :, 0]), pad_win(win_rows[:, :, 1]), cover, expand)
    return jnp.transpose(out, (0, 3, 1, 2, 4))


def rms_norm(x, g):
    xf = x.astype(jnp.float32)
    y = xf * lax.rsqrt(jnp.mean(xf * xf, -1, keepdims=True) + NORM_EPS)
    return (y * g.astype(jnp.float32)).astype(x.dtype)


def rope(x, pos):
    half = x.shape[-1] // 2
    inv = ROPE_THETA ** (-jnp.arange(half, dtype=jnp.float32) / half)
    ang = pos.astype(jnp.float32)[:, None] * inv[None, :]
    shp = (pos.shape[0],) + (1,) * (x.ndim - 3) + (half,)
    cos, sin = jnp.cos(ang).reshape(shp), jnp.sin(ang).reshape(shp)
    xf = x.astype(jnp.float32)
    x1, x2 = xf[..., :half], xf[..., half:]
    return jnp.concatenate([x1 * cos - x2 * sin, x2 * cos + x1 * sin], -1).astype(x.dtype)


def masked_softmax(s, mask):
    s = jnp.where(mask, s.astype(jnp.float32), -jnp.inf)
    m = jnp.max(s, -1, keepdims=True)
    m = jnp.where(jnp.isfinite(m), m, 0.0)
    p = jnp.exp(s - m)
    den = jnp.sum(p, -1, keepdims=True)
    return p / jnp.where(den > 0, den, 1.0)


def map_query_blocks(fn, qs, q_pos, blk):
    T = q_pos.shape[0]
    if T <= blk:
        return fn(qs, q_pos)
    nb = -(-T // blk)
    pad = nb * blk - T

    def split(a):
        a = jnp.pad(a, [(0, 0), (0, pad)] + [(0, 0)] * (a.ndim - 2))
        return jnp.moveaxis(a.reshape(a.shape[0], nb, blk, *a.shape[2:]), 1, 0)

    qb = tuple(split(a) for a in qs)
    pb = jnp.pad(q_pos, (0, pad), mode='edge').reshape(nb, blk)
    out = lax.map(lambda a: fn(a[0], a[1]), (qb, pb))
    out = jnp.moveaxis(out, 0, 1)
    return out.reshape(out.shape[0], nb * blk, *out.shape[3:])[:, :T]


def mla_project(h, pos, w_in, q_norm, kv_norm, w_uq, w_uk):
    z = h @ w_in
    cq = rms_norm(z[..., :MLA_Q_LORA], q_norm)
    ckv = rms_norm(z[..., MLA_Q_LORA:MLA_Q_LORA + MLA_KV_LORA], kv_norm)
    kr = rope(z[..., MLA_Q_LORA + MLA_KV_LORA:][:, :, None, :], pos)[:, :, 0]
    lat = jnp.concatenate([ckv, kr], -1)
    q = jnp.einsum('btr,rhd->bthd', cq, w_uq)
    q_lat = jnp.einsum('bthn,rhn->bthr', q[..., :MLA_NOPE], w_uk)
    q_abs = jnp.concatenate([q_lat, rope(q[..., MLA_NOPE:], pos)], -1)
    return lat, q_abs


def latent_attend(q_abs, q_pos, lat, k_pos):
    s = jnp.einsum('bthc,bsc->bhts', q_abs, lat) * MLA_SCALE
    p = masked_softmax(s, (k_pos[None, :] <= q_pos[:, None])[None, None])
    return jnp.einsum('bhts,bsr->bthr', p.astype(lat.dtype), lat[..., :MLA_KV_LORA])


def mla_out(o_lat, w_uv, w_o):
    B, T = o_lat.shape[:2]
    o = jnp.einsum('bthr,rhv->bthv', o_lat, w_uv)
    return o.reshape(B, T, MLA_HEADS * MLA_V) @ w_o


def mla_prompt(h, w_in, q_norm, kv_norm, w_uq, w_uk, w_uv, w_o):
    T = h.shape[1]
    pos = jnp.arange(T, dtype=jnp.int32)
    lat, q_abs = mla_project(h, pos, w_in, q_norm, kv_norm, w_uq, w_uk)
    o = map_query_blocks(lambda qs, p: latent_attend(qs[0], p, lat, pos), (q_abs,), pos, Q_BLOCK)
    return mla_out(o, w_uv, w_o), lat


def mla_sample(h, pool, page_table, w_in, q_norm, kv_norm, w_uq, w_uk, w_uv, w_o):
    B, T = h.shape[:2]
    past = page_table.shape[1] * PAGE_SIZE
    pos = past + jnp.arange(T, dtype=jnp.int32)
    lat_new, q_abs = mla_project(h, pos, w_in, q_norm, kv_norm, w_uq, w_uk)
    lat = jnp.concatenate([pool[page_table].reshape(B, past, MLA_LAT), lat_new], 1)
    k_pos = jnp.arange(past + T, dtype=jnp.int32)
    o = map_query_blocks(lambda qs, p: latent_attend(qs[0], p, lat, k_pos), (q_abs,), pos, Q_BLOCK)
    return mla_out(o, w_uv, w_o), lat_new


def nsa_project(h, pos, w_in, b_gate):
    B, T = h.shape[:2]
    nq = NSA_HEADS * NSA_HD
    nkv = 6 * NSA_KV_HEADS * NSA_HD
    z = h @ w_in
    q = z[..., :nq].reshape(B, T, NSA_KV_HEADS, NSA_GROUP, NSA_HD)
    kv = z[..., nq:nq + nkv].reshape(B, T, 3, 2, NSA_KV_HEADS, NSA_HD)
    gates = jax.nn.sigmoid(z[..., nq + nkv:] + b_gate).reshape(B, T, NSA_KV_HEADS, NSA_GROUP, 3)
    q_rot = rope(q, pos)

    def rot_k(r):
        return jnp.stack([rope(r[:, :, 0], pos), r[:, :, 1]], axis=2)

    return q, q_rot, gates, kv[:, :, 0], rot_k(kv[:, :, 1]), rot_k(kv[:, :, 2])


def nsa_compress(rows, w1, b1, w2):
    B, T = rows.shape[:2]
    nch = -(-T // NSA_CMP_STRIDE)
    rows = jnp.pad(rows, ((0, 0), (0, nch * NSA_CMP_STRIDE - T), (0, 0), (0, 0)))
    ch = rows.reshape(B, nch, NSA_CMP_STRIDE, NSA_KV_HEADS, NSA_HD)
    nb = nch - NSA_CMP_RATIO + 1
    pre = b1
    for r in range(NSA_CMP_RATIO):
        w1r = w1[r * NSA_CMP_STRIDE:(r + 1) * NSA_CMP_STRIDE]
        pre = pre + jnp.einsum('bclgd,ldh->bcgh', ch[:, r:r + nb], w1r)
    return jnp.einsum('bngh,hd->bngd', jax.nn.gelu(pre), w2)


def nsa_core(q_raw, q_rot, gates, q_pos, kc, vc, gather_slc, n_blocks, kw, vw, kw_pos):
    nb = kc.shape[1]
    cmp_end = jnp.arange(nb, dtype=jnp.int32) * NSA_CMP_STRIDE + NSA_CMP_LEN - 1
    s_c = jnp.einsum('btghd,bngd->btghn', q_raw, kc) * NSA_SCALE
    p_c = masked_softmax(s_c, (cmp_end[None, :] <= q_pos[:, None])[None, :, None, None, :])
    o_c = jnp.einsum('btghn,bngd->btghd', p_c.astype(vc.dtype), vc)
    ci = jnp.arange(nb, dtype=jnp.int32)[:, None] * NSA_CMP_STRIDE
    sj = jnp.arange(n_blocks, dtype=jnp.int32)[None, :] * NSA_SLC_LEN
    cover = ((ci < sj + NSA_SLC_LEN) & (ci + NSA_CMP_LEN > sj)).astype(jnp.float32)
    imp = jnp.einsum('btghn,nj->btgj', p_c, cover)
    j = jnp.arange(n_blocks, dtype=jnp.int32)[None, :]
    cur = (q_pos // NSA_SLC_LEN)[:, None]
    valid = j * NSA_SLC_LEN <= q_pos[:, None]
    forced = (j == 0) | (j == cur) | (j == cur - 1)
    score = jnp.where((valid & forced)[None, :, None, :], NSA_FORCE,
                      jnp.where(valid[None, :, None, :], imp, -jnp.inf))
    _, idx = lax.top_k(score, min(NSA_TOPN, n_blocks))
    sel = gather_slc(idx)
    ks, vs = sel[..., 0, :], sel[..., 1, :]
    k_pos = idx[..., None] * NSA_SLC_LEN + jnp.arange(NSA_SLC_LEN, dtype=jnp.int32)
    s_s = jnp.einsum('btghd,btgnld->btghnl', q_rot, ks) * NSA_SCALE
    B, Tq, G, H, n, L = s_s.shape
    m_s = (k_pos <= q_pos[None, :, None, None, None]).reshape(B, Tq, G, 1, n * L)
    p_s = masked_softmax(s_s.reshape(B, Tq, G, H, n * L), m_s).reshape(s_s.shape)
    o_s = jnp.einsum('btghnl,btgnld->btghd', p_s.astype(vs.dtype), vs)
    s_w = jnp.einsum('btghd,bsgd->btghs', q_rot, kw) * NSA_SCALE
    dpos = q_pos[:, None] - kw_pos[None, :]
    m_w = (dpos >= 0) & (dpos < NSA_WINDOW) & (kw_pos[None, :] >= 0)
    p_w = masked_softmax(s_w, m_w[None, :, None, None, :])
    o_w = jnp.einsum('btghs,bsgd->btghd', p_w.astype(vw.dtype), vw)
    return gates[..., 0:1] * o_c + gates[..., 1:2] * o_s + gates[..., 2:3] * o_w


def _bg(B):
    return (jnp.arange(B)[:, None, None, None], jnp.arange(NSA_KV_HEADS)[None, None, :, None])


def nsa_prompt(h, w_in, b_gate, w1, b1, w2, w_o):
    B, T = h.shape[:2]
    pos = jnp.arange(T, dtype=jnp.int32)
    q, q_rot, gates, cmp_rows, slc_rows, win_rows = nsa_project(h, pos, w_in, b_gate)
    kc = nsa_compress(cmp_rows[:, :, 0], w1[0], b1[0], w2[0])
    vc = nsa_compress(cmp_rows[:, :, 1], w1[1], b1[1], w2[1])
    o = nsa_prompt_attend_pallas(q, q_rot, gates, kc, vc, slc_rows, win_rows)
    y = o.reshape(B, T, NSA_HEADS * NSA_HD) @ w_o
    wb = min(NSA_WINDOW, T)
    return y, cmp_rows, slc_rows, win_rows[:, T - wb:]


def nsa_sample(h, pool_cmp, pool_slc, win_state, page_table, w_in, b_gate, w1, b1, w2, w_o):
    B, T = h.shape[:2]
    past = page_table.shape[1] * PAGE_SIZE
    pos = past + jnp.arange(T, dtype=jnp.int32)
    q, q_rot, gates, cmp_new, slc_new, win_new = nsa_project(h, pos, w_in, b_gate)
    cmp_all = jnp.concatenate([pool_cmp[page_table].reshape(B, past, 2, NSA_KV_HEADS, NSA_HD), cmp_new], 1)
    kc = nsa_compress(cmp_all[:, :, 0], w1[0], b1[0], w2[0])
    vc = nsa_compress(cmp_all[:, :, 1], w1[1], b1[1], w2[1])
    spp = PAGE_SIZE // NSA_SLC_LEN
    pool_blocks = pool_slc.reshape(pool_slc.shape[0] * spp, NSA_SLC_LEN, 2, NSA_KV_HEADS, NSA_HD)
    ns_past = past // NSA_SLC_LEN
    n_new = -(-T // NSA_SLC_LEN)
    new_blocks = jnp.pad(slc_new, ((0, 0), (0, n_new * NSA_SLC_LEN - T), (0, 0), (0, 0), (0, 0)))
    new_blocks = new_blocks.reshape(B, n_new, NSA_SLC_LEN, 2, NSA_KV_HEADS, NSA_HD)
    bi, gi = _bg(B)

    def gather(idx):
        jp = jnp.minimum(idx, ns_past - 1)
        phys = page_table[bi, jp // spp] * spp + jp % spp
        from_past = pool_blocks[phys, :, :, gi, :]
        jn = jnp.clip(idx - ns_past, 0, n_new - 1)
        from_new = new_blocks[bi, jn, :, :, gi, :]
        return jnp.where((idx < ns_past)[..., None, None, None], from_past, from_new)

    wb = win_state.shape[1]
    win_all = jnp.concatenate([win_state, win_new], 1)
    kw_pos = past - wb + jnp.arange(wb + T, dtype=jnp.int32)
    o = map_query_blocks(
        lambda qs, p: nsa_core(qs[0], qs[1], qs[2], p, kc, vc, gather, ns_past + n_new,
                               win_all[:, :, 0], win_all[:, :, 1], kw_pos),
        (q, q_rot, gates), pos, NSA_Q_BLOCK)
    y = o.reshape(B, T, NSA_HEADS * NSA_HD) @ w_o
    return y, cmp_new, slc_new, win_all[:, T:]


def peer(h, w_q, sub_keys, u, v):
    B, T, D = h.shape
    n = B * T
    x = h.reshape(n, D)
    idx, g = peer_route_pallas(x, w_q, sub_keys)
    out = peer_experts_pallas(x, idx, g, u, v)
    return out.reshape(B, T, D)


def kernel(x_prompt, x_sample, cache_mla_latent, cache_nsa_cmp, cache_nsa_slc, state_nsa_win, page_table,
           norm_mix, norm_ffn, norm_final, mla_w_in, mla_q_norm, mla_kv_norm, mla_w_uq, mla_w_uk, mla_w_uv,
           mla_w_o, nsa_w_in, nsa_b_gate, nsa_phi_w1, nsa_phi_b1, nsa_phi_w2, nsa_w_o, peer_w_q, peer_sub_keys,
           peer_u, peer_v):
    depth = norm_mix.shape[0]
    xp, xs = x_prompt, x_sample
    mla_p, mla_s, cmp_p, cmp_s, slc_p, slc_s, win_p, win_s = [], [], [], [], [], [], [], []
    for i in range(depth):
        j = i // N_MIXERS
        hp = rms_norm(xp, norm_mix[i])
        hs = rms_norm(xs, norm_mix[i])
        if i % N_MIXERS == 0:
            w = (mla_w_in[j], mla_q_norm[j], mla_kv_norm[j], mla_w_uq[j], mla_w_uk[j], mla_w_uv[j], mla_w_o[j])
            op, rp = mla_prompt(hp, *w)
            osm, rs = mla_sample(hs, cache_mla_latent[j], page_table, *w)
            mla_p.append(rp)
            mla_s.append(rs)
        else:
            w = (nsa_w_in[j], nsa_b_gate[j], nsa_phi_w1[j], nsa_phi_b1[j], nsa_phi_w2[j], nsa_w_o[j])
            op, cp, sp, wp = nsa_prompt(hp, *w)
            osm, cs, ss, ws = nsa_sample(hs, cache_nsa_cmp[j], cache_nsa_slc[j], state_nsa_win[j], page_table, *w)
            cmp_p.append(cp)
            cmp_s.append(cs)
            slc_p.append(sp)
            slc_s.append(ss)
            win_p.append(wp)
            win_s.append(ws)
        xp = xp + op
        xs = xs + osm
        pw = (peer_w_q[i], peer_sub_keys[i], peer_u[i], peer_v[i])
        xp = xp + peer(rms_norm(xp, norm_ffn[i]), *pw)
        xs = xs + peer(rms_norm(xs, norm_ffn[i]), *pw)
    y_prompt = rms_norm_pallas(xp, norm_final)
    y_sample = rms_norm_pallas(xs, norm_final)
    return (y_prompt, y_sample, jnp.stack(mla_p), jnp.stack(mla_s), jnp.stack(cmp_p), jnp.stack(cmp_s),
            jnp.stack(slc_p), jnp.stack(slc_s), jnp.stack(win_p), jnp.stack(win_s))
```
